```python
import math
import jax
import jax.numpy as jnp
from jax import lax
import numpy as np

D_MODEL = 4096
BATCH = 4
SEQ = 2048
DEPTH = 4
DEC_BATCH = 128
DEC_SEQ = 1
PAST_LEN = 16384
PAGE_SIZE = 128

N_MEM = 256
RET_HEADS = 4
RET_DK = D_MODEL // 32
RET_DV = D_MODEL // 16
RET_QK_W = RET_HEADS * RET_DK
RET_W = RET_HEADS * RET_DV
POOL_WINDOWS = (2, 4, 8, 16)
POOL_GROUP = D_MODEL // 16
POOL_W = POOL_GROUP * len(POOL_WINDOWS)
POOL_BUF = max(POOL_WINDOWS) - 1
GDN_HEADS = 16
GDN_DK = D_MODEL // 32
GDN_DV = D_MODEL // 32
GDN_QK_W = GDN_HEADS * GDN_DK
GDN_W = GDN_HEADS * GDN_DV
GDN_CONV_CH = 2 * GDN_QK_W + GDN_W
GDN_CONV = 4
MIX_W = RET_W + POOL_W + GDN_W
X_HEADS = 4
X_HD = D_MODEL // 16
X_W = X_HEADS * X_HD
D_FF = ((8 * D_MODEL // 3 + 255) // 256) * 256
FFN_CONV = 3
CHUNK = 64
ROPE_THETA = 10000.0
EPS = 1e-6

OFF_RQ = 0
OFF_RK = OFF_RQ + RET_QK_W
OFF_RV = OFF_RK + RET_QK_W
OFF_RG = OFF_RV + RET_W
OFF_PU = OFF_RG + RET_W
OFF_CQKV = OFF_PU + POOL_W
OFF_CZ = OFF_CQKV + GDN_CONV_CH
OFF_CB = OFF_CZ + GDN_W
OFF_CA = OFF_CB + GDN_HEADS
N_IN = OFF_CA + GDN_HEADS

kernel_name = 'hybrid_retention_pool_gdn_decoder_step'


def rmsnorm(x, g):
    xf = x.astype(jnp.float32)
    y = xf * lax.rsqrt(jnp.mean(xf * xf, axis=-1, keepdims=True) + EPS)
    return (y * g.astype(jnp.float32)).astype(x.dtype)


def rms_unit(x):
    xf = x.astype(jnp.float32)
    return (xf * lax.rsqrt(jnp.mean(xf * xf, axis=-1, keepdims=True) + EPS)).astype(x.dtype)


def l2norm(x):
    xf = x.astype(jnp.float32)
    return xf * lax.rsqrt(jnp.sum(xf * xf, axis=-1, keepdims=True) + EPS)


def rope(x, pos):
    half = x.shape[-1] // 2
    inv = ROPE_THETA ** (-jnp.arange(half, dtype=jnp.float32) / half)
    ang = pos.astype(jnp.float32)[:, None] * inv[None, :]
    cos = jnp.cos(ang)[None, :, None, :]
    sin = jnp.sin(ang)[None, :, None, :]
    xf = x.astype(jnp.float32)
    x1, x2 = xf[..., :half], xf[..., half:]
    return jnp.concatenate([x1 * cos - x2 * sin, x1 * sin + x2 * cos], axis=-1).astype(x.dtype)


def causal_dwconv(u, buf, w):
    width = w.shape[0]
    T = u.shape[1]
    ext = jnp.concatenate([buf.astype(u.dtype), u], axis=1)
    y = ext[:, 0:T] * w[0]
    for i in range(1, width):
        y = y + ext[:, i:i + T] * w[i]
    return y, ext[:, T:]


def _chunk_layout(T):
    c = min(CHUNK, T)
    n = -(-T // c)
    return c, n, n * c - T


def _pad_time(a, pad):
    return jnp.pad(a, [(0, 0), (0, pad)] + [(0, 0)] * (a.ndim - 2))


def _to_chunks(a, n, c):
    B, _, H = a.shape[:3]
    a = a.reshape((B, n, c, H) + a.shape[3:])
    return a.transpose((1, 0, 3, 2) + tuple(range(4, a.ndim)))


def _from_chunks(o, T):
    n, B, H, c, d = o.shape
    return o.transpose(1, 0, 3, 2, 4).reshape(B, n * c, H, d)[:, :T]


def retention_chunked(q, k, v, log_gamma, s0):
    B, T, H, _ = q.shape
    c, n, pad = _chunk_layout(T)
    f = jnp.float32
    qc = _to_chunks(_pad_time(q.astype(f), pad), n, c)
    kc = _to_chunks(_pad_time(k.astype(f), pad), n, c)
    vc = _to_chunks(_pad_time(v.astype(f), pad), n, c)
    valid = jnp.arange(n * c) < T
    g = jnp.where(valid[:, None], log_gamma[None, :], 0.0).reshape(n, c, H).transpose(0, 2, 1)
    b = jnp.cumsum(g, axis=-1)
    causal = jnp.tril(jnp.ones((c, c), bool))
    dmask = jnp.exp(jnp.where(causal, b[..., :, None] - b[..., None, :], -jnp.inf))

    def step(S, inp):
        qi, ki, vi, bi, di = inp
        scores = jnp.einsum('bhid,bhjd->bhij', qi, ki) * di[None]
        o = (jnp.einsum('bhij,bhje->bhie', scores, vi)
             + jnp.einsum('bhid,bhde->bhie', qi * jnp.exp(bi)[None, :, :, None], S))
        bl = bi[:, -1]
        S = (S * jnp.exp(bl)[None, :, None, None]
             + jnp.einsum('bhjd,bhje->bhde', ki * jnp.exp(bl[:, None] - bi)[None, :, :, None], vi))
        return S, o

    S, o = lax.scan(step, s0.astype(f), (qc, kc, vc, b, dmask))
    return _from_chunks(o, T).astype(v.dtype), S.astype(s0.dtype)


def gated_delta_chunked(q, k, v, beta, g, s0):
    B, T, H, _ = q.shape
    c, n, pad = _chunk_layout(T)
    f = jnp.float32
    qc = _to_chunks(_pad_time(q.astype(f), pad), n, c)
    kc = _to_chunks(_pad_time(k.astype(f), pad), n, c)
    vc = _to_chunks(_pad_time(v.astype(f), pad), n, c)
    bc = _to_chunks(_pad_time(beta.astype(f), pad), n, c)
    gc = _to_chunks(_pad_time(g.astype(f), pad), n, c)
    b = jnp.cumsum(gc, axis=-1)
    causal = jnp.tril(jnp.ones((c, c), bool))
    strict = jnp.tril(jnp.ones((c, c), bool), -1)
    decay = jnp.exp(jnp.where(causal, b[..., :, None] - b[..., None, :], -jnp.inf))
    kb = kc * bc[..., None]
    lower = jnp.where(strict, jnp.einsum('nbhid,nbhjd->nbhij', kb, kc) * decay, 0.0)
    eye = jnp.eye(c, dtype=f)
    a = eye + lower
    tinv = lax.linalg.triangular_solve(a, jnp.broadcast_to(eye, a.shape), left_side=True,
                                       lower=True, unit_diagonal=True)
    u = jnp.einsum('nbhij,nbhje->nbhie', tinv, vc * bc[..., None])
    w = jnp.einsum('nbhij,nbhjd->nbhid', tinv, kb * jnp.exp(b)[..., None])
    attn = jnp.einsum('nbhid,nbhjd->nbhij', qc, kc) * decay

    def step(S, inp):
        qi, ki, ui, wi, bi, ai = inp
        v_new = ui - jnp.einsum('bhid,bhde->bhie', wi, S)
        o = (jnp.einsum('bhid,bhde->bhie', qi * jnp.exp(bi)[..., None], S)
             + jnp.einsum('bhij,bhje->bhie', ai, v_new))
        bl = bi[..., -1]
        S = (S * jnp.exp(bl)[..., None, None]
             + jnp.einsum('bhjd,bhje->bhde', ki * jnp.exp(bl[..., None] - bi)[..., None], v_new))
        return S, o

    S, o = lax.scan(step, s0.astype(f), (qc, kc, u, w, b, attn))
    return _from_chunks(o, T), S.astype(s0.dtype)


def pool_mixer(u, buf, pos0, w_pool, ls):
    B, T, C = u.shape
    ext = jnp.concatenate([buf.astype(u.dtype), u], axis=1)
    ef = ext.astype(jnp.float32)
    cs = jnp.concatenate([jnp.zeros((B, 1, C), jnp.float32), jnp.cumsum(ef, axis=1)], axis=1)
    hi = cs[:, POOL_BUF + 1:]
    cur = ef[:, POOL_BUF:]
    pos = pos0 + jnp.arange(T, dtype=jnp.int32)
    outs = []
    for gi, win in enumerate(POOL_WINDOWS):
        sl = slice(gi * POOL_GROUP, (gi + 1) * POOL_GROUP)
        lo = cs[:, POOL_BUF + 1 - win:POOL_BUF + 1 - win + T, sl]
        cnt = jnp.minimum(pos + 1, win).astype(jnp.float32)[None, :, None]
        d = ((hi[..., sl] - lo) / cnt - cur[..., sl]).astype(u.dtype)
        outs.append(d @ w_pool[gi])
    return jnp.concatenate(outs, axis=-1) * ls, ext[:, T:]


def token_mixer(h, pos0, s_ret, s_pool, s_gdn, s_gconv, P, l):
    B, T, _ = h.shape
    z = h @ P['w_in'][l]
    pos = pos0 + jnp.arange(T, dtype=jnp.int32)
    rq = rope(z[..., OFF_RQ:OFF_RK].reshape(B, T, RET_HEADS, RET_DK), pos)
    rk = rope(z[..., OFF_RK:OFF_RV].reshape(B, T, RET_HEADS, RET_DK), pos) * (RET_DK ** -0.5)
    rv = z[..., OFF_RV:OFF_RG].reshape(B, T, RET_HEADS, RET_DV)
    log_gamma = jnp.log(1.0 - jnp.exp2(-5.0 - jnp.arange(RET_HEADS, dtype=jnp.float32)))
    ro, s_ret_new = retention_chunked(rq, rk, rv, log_gamma, s_ret)
    ro = rms_unit(ro).reshape(B, T, RET_W) * jax.nn.silu(z[..., OFF_RG:OFF_PU])
    po, s_pool_new = pool_mixer(z[..., OFF_PU:OFF_CQKV], s_pool, pos0, P['w_pool'][l], P['ls_pool'][l])
    cu, s_gconv_new = causal_dwconv(z[..., OFF_CQKV:OFF_CZ], s_gconv, P['w_gconv'][l])
    cu = jax.nn.silu(cu)
    cq = l2norm(cu[..., :GDN_QK_W].reshape(B, T, GDN_HEADS, GDN_DK)) * (GDN_DK ** -0.5)
    ck = l2norm(cu[..., GDN_QK_W:2 * GDN_QK_W].reshape(B, T, GDN_HEADS, GDN_DK))
    cv = cu[..., 2 * GDN_QK_W:].reshape(B, T, GDN_HEADS, GDN_DV)
    beta = jax.nn.sigmoid(z[..., OFF_CB:OFF_CA].astype(jnp.float32))
    g = -jnp.exp(P['gdn_a_log'][l].astype(jnp.float32)) * jax.nn.softplus(
        z[..., OFF_CA:N_IN].astype(jnp.float32) + P['gdn_dt_bias'][l].astype(jnp.float32))
    co, s_gdn_new = gated_delta_chunked(cq, ck, cv, beta, g, s_gdn)
    cz = z[..., OFF_CZ:OFF_CB].reshape(B, T, GDN_HEADS, GDN_DV).astype(jnp.float32)
    co = (rmsnorm(co, P['gdn_norm'][l]) * jax.nn.silu(cz)).astype(h.dtype).reshape(B, T, GDN_W)
    y = jnp.concatenate([ro, po.astype(h.dtype), co], axis=-1) @ P['w_out'][l]
    return y, s_ret_new, s_pool_new, s_gdn_new, s_gconv_new


def mem_kv(mem, P, l):
    Bm, M, _ = mem.shape
    mn = rmsnorm(mem, P['norm_mem'][l])
    k = (mn @ P['w_xk'][l]).reshape(Bm, M, X_HEADS, X_HD)
    v = (mn @ P['w_xv'][l]).reshape(Bm, M, X_HEADS, X_HD)
    return k, v


def cross_attn(h, mk, mv, P, l):
    B, T, _ = h.shape
    q = (h @ P['w_xq'][l]).reshape(B, T, X_HEADS, X_HD)
    s = jnp.einsum('bthd,bmhd->bhtm', q, mk.astype(q.dtype)).astype(jnp.float32) * (X_HD ** -0.5)
    a = jax.nn.softmax(s, axis=-1).astype(h.dtype)
    o = jnp.einsum('bhtm,bmhd->bthd', a, mv.astype(h.dtype)).reshape(B, T, X_W)
    return o @ P['w_xo'][l]


def conv_ffn(h, s_fconv, P, l):
    a, buf = causal_dwconv(h @ P['w_gate'][l], s_fconv, P['w_fconv'][l])
    a = a + P['b_fconv'][l]
    y = (jax.nn.gelu(a, approximate=False) * (h @ P['w_up'][l])) @ P['w_down'][l]
    return y, buf


def run_trunk(x, pos0, st_ret, st_pool, st_gdn, st_gconv, st_fconv, mem_k, mem_v, P):
    outs = [[], [], [], [], []]
    for l in range(DEPTH):
        m, r_new, p_new, g_new, gc_new = token_mixer(rmsnorm(x, P['norm_mix'][l]), pos0, st_ret[l],
                                                     st_pool[l], st_gdn[l], st_gconv[l], P, l)
        x = x + m
        x = x + cross_attn(rmsnorm(x, P['norm_x'][l]), mem_k[l], mem_v[l], P, l)
        f, f_new = conv_ffn(rmsnorm(x, P['norm_ffn'][l]), st_fconv[l], P, l)
        x = x + f
        for lst, s in zip(outs, (r_new, p_new, g_new, gc_new, f_new)):
            lst.append(s)
    return rmsnorm(x, P['norm_f']), [jnp.stack(s) for s in outs]


def setup_inputs(seed: int = 0) -> dict:
    key = jax.random.key(seed)
    ks = iter(jax.random.split(key, 48))
    f = jnp.float32

    def nrm(shape, scale):
        return jax.random.normal(next(ks), shape, f) * scale

    def gain(shape):
        return 1.0 + 0.02 * jax.random.normal(next(ks), shape, f)

    x_prompt = nrm((BATCH, SEQ, D_MODEL), 1.0)
    x_sample = nrm((DEC_BATCH, DEC_SEQ, D_MODEL), 1.0)
    state_ret = nrm((DEPTH, DEC_BATCH, RET_HEADS, RET_DK, RET_DV), 0.5)
    state_pool = nrm((DEPTH, DEC_BATCH, POOL_BUF, POOL_W), 1.0)
    state_gdn = nrm((DEPTH, DEC_BATCH, GDN_HEADS, GDN_DK, GDN_DV), 0.3)
    state_gdn_conv = nrm((DEPTH, DEC_BATCH, GDN_CONV - 1, GDN_CONV_CH), 1.0)
    state_ffn_conv = nrm((DEPTH, DEC_BATCH, FFN_CONV - 1, D_FF), 1.0)
    cache_mem_k = nrm((DEPTH, DEC_BATCH, N_MEM, X_HEADS, X_HD), 1.0)
    cache_mem_v = nrm((DEPTH, DEC_BATCH, N_MEM, X_HEADS, X_HD), 1.0)
    mem_prompt = nrm((BATCH, N_MEM, D_MODEL), 1.0)
    norm_mix = gain((DEPTH, D_MODEL))
    w_in = nrm((DEPTH, D_MODEL, N_IN), D_MODEL ** -0.5)
    w_pool = nrm((DEPTH, len(POOL_WINDOWS), POOL_GROUP, POOL_GROUP), POOL_GROUP ** -0.5)
    ls_pool = gain((DEPTH, POOL_W))
    w_gconv = nrm((DEPTH, GDN_CONV, GDN_CONV_CH), GDN_CONV ** -0.5)
    gdn_a_log = jnp.log(jax.random.uniform(next(ks), (DEPTH, GDN_HEADS), f, 1.0, 16.0))
    dt = jnp.exp(jax.random.uniform(next(ks), (DEPTH, GDN_HEADS), f, math.log(1e-3), math.log(1e-1)))
    gdn_dt_bias = dt + jnp.log(-jnp.expm1(-dt))
    gdn_norm = gain((DEPTH, GDN_DV))
    w_out = nrm((DEPTH, MIX_W, D_MODEL), MIX_W ** -0.5)
    norm_x = gain((DEPTH, D_MODEL))
    norm_mem = gain((DEPTH, D_MODEL))
    w_xq = nrm((DEPTH, D_MODEL, X_W), D_MODEL ** -0.5)
    w_xk = nrm((DEPTH, D_MODEL, X_W), D_MODEL ** -0.5)
    w_xv = nrm((DEPTH, D_MODEL, X_W), D_MODEL ** -0.5)
    w_xo = nrm((DEPTH, X_W, D_MODEL), X_W ** -0.5)
    norm_ffn = gain((DEPTH, D_MODEL))
    w_gate = nrm((DEPTH, D_MODEL, D_FF), D_MODEL ** -0.5)
    w_up = nrm((DEPTH, D_MODEL, D_FF), D_MODEL ** -0.5)
    w_fconv = nrm((DEPTH, FFN_CONV, D_FF), FFN_CONV ** -0.5)
    b_fconv = nrm((DEPTH, D_FF), 0.02)
    w_down = nrm((DEPTH, D_FF, D_MODEL), D_FF ** -0.5)
    norm_f = gain((D_MODEL,))
    return {'x_prompt': x_prompt, 'x_sample': x_sample, 'state_ret': state_ret,
            'state_pool': state_pool, 'state_gdn': state_gdn, 'state_gdn_conv': state_gdn_conv,
            'state_ffn_conv': state_ffn_conv, 'cache_mem_k': cache_mem_k, 'cache_mem_v': cache_mem_v,
            'mem_prompt': mem_prompt, 'norm_mix': norm_mix, 'w_in': w_in, 'w_pool': w_pool,
            'ls_pool': ls_pool, 'w_gconv': w_gconv, 'gdn_a_log': gdn_a_log, 'gdn_dt_bias': gdn_dt_bias,
            'gdn_norm': gdn_norm, 'w_out': w_out, 'norm_x': norm_x, 'norm_mem': norm_mem,
            'w_xq': w_xq, 'w_xk': w_xk, 'w_xv': w_xv, 'w_xo': w_xo, 'norm_ffn': norm_ffn,
            'w_gate': w_gate, 'w_up': w_up, 'w_fconv': w_fconv, 'b_fconv': b_fconv,
            'w_down': w_down, 'norm_f': norm_f}


def reference(x_prompt, x_sample, state_ret, state_pool, state_gdn, state_gdn_conv, state_ffn_conv,
              cache_mem_k, cache_mem_v, mem_prompt, norm_mix, w_in, w_pool, ls_pool, w_gconv,
              gdn_a_log, gdn_dt_bias, gdn_norm, w_out, norm_x, norm_mem, w_xq, w_xk, w_xv, w_xo,
              norm_ffn, w_gate, w_up, w_fconv, b_fconv, w_down, norm_f):
    P = dict(norm_mix=norm_mix, w_in=w_in, w_pool=w_pool, ls_pool=ls_pool, w_gconv=w_gconv,
             gdn_a_log=gdn_a_log, gdn_dt_bias=gdn_dt_bias, gdn_norm=gdn_norm, w_out=w_out,
             norm_x=norm_x, norm_mem=norm_mem, w_xq=w_xq, w_xk=w_xk, w_xv=w_xv, w_xo=w_xo,
             norm_ffn=norm_ffn, w_gate=w_gate, w_up=w_up, w_fconv=w_fconv, b_fconv=b_fconv,
             w_down=w_down, norm_f=norm_f)
    dt = x_prompt.dtype
    bp = x_prompt.shape[0]

    def zeros(*s):
        return jnp.zeros((DEPTH, bp) + s, dt)

    pk, pv = [], []
    for l in range(DEPTH):
        k_l, v_l = mem_kv(mem_prompt, P, l)
        pk.append(k_l)
        pv.append(v_l)
    p_mem_k = jnp.stack(pk)
    p_mem_v = jnp.stack(pv)
    y_prompt, p_st = run_trunk(x_prompt, 0,
                               zeros(RET_HEADS, RET_DK, RET_DV), zeros(POOL_BUF, POOL_W),
                               zeros(GDN_HEADS, GDN_DK, GDN_DV), zeros(GDN_CONV - 1, GDN_CONV_CH),
                               zeros(FFN_CONV - 1, D_FF), p_mem_k, p_mem_v, P)
    p_ret, p_pool, p_gdn, p_gconv, p_fconv = p_st
    y_sample, s_st = run_trunk(x_sample, PAST_LEN, state_ret, state_pool, state_gdn, state_gdn_conv,
                               state_ffn_conv, cache_mem_k, cache_mem_v, P)
    s_ret, s_pool, s_gdn, s_gconv, s_fconv = s_st
    return (y_prompt, y_sample, p_ret, p_pool, p_gdn, p_gconv, p_fconv, p_mem_k, p_mem_v,
            s_ret, s_pool, s_gdn, s_gconv, s_fconv)
```

```python
import functools
import math

import jax
import jax.numpy as jnp
from jax import lax
from jax.experimental import pallas as pl
from jax.experimental.pallas import tpu as pltpu

F32 = jnp.float32
BF16 = jnp.bfloat16

EPS = 1e-6
ROPE_THETA = 10000.0
RET_HEADS = 4
GDN_HEADS = 16
X_HEADS = 4
POOL_WINDOWS = (2, 4, 8, 16)
POOL_BUF = max(POOL_WINDOWS) - 1
GDN_CONV = 4
FFN_CONV = 3
PAST_LEN = 16384

V7X_VMEM_BYTES = 64 * 1024 * 1024
VMEM_LIMIT = V7X_VMEM_BYTES - 8 * 1024 * 1024
LANES = 128
GDN_CHUNK = 64
GDN_SUB = 16
GDN_HB = 4
NEG_BIG = -1e30


def _cparams(*sem):
    return pltpu.CompilerParams(dimension_semantics=sem, vmem_limit_bytes=VMEM_LIMIT)


def _dot(a, b):
    return jnp.dot(a, b, preferred_element_type=F32)


def _dot_nt(a, b):
    return lax.dot_general(a, b, (((1,), (1,)), ((), ())), preferred_element_type=F32)


def _dot_tn(a, b):
    return lax.dot_general(a, b, (((0,), (0,)), ((), ())), preferred_element_type=F32)


def _silu(x):
    return x * (1.0 / (1.0 + jnp.exp(-x)))


def _blk(n, want):
    b = min(n, want)
    while n % b:
        b //= 2
    return b


def _rmsnorm_body(x_ref, g_ref, o_ref):
    x = x_ref[...]
    y = x * lax.rsqrt(jnp.mean(x * x, axis=-1, keepdims=True) + EPS)
    o_ref[...] = (y * g_ref[...]).astype(o_ref.dtype)


def rmsnorm(x, g, out_dtype=BF16):
    M, D = x.shape
    bm = _blk(M, 256)
    return pl.pallas_call(
        _rmsnorm_body, grid=(M // bm,),
        in_specs=[pl.BlockSpec((bm, D), lambda i: (i, 0)), pl.BlockSpec((1, D), lambda i: (0, 0))],
        out_specs=pl.BlockSpec((bm, D), lambda i: (i, 0)),
        out_shape=jax.ShapeDtypeStruct((M, D), out_dtype),
        compiler_params=_cparams("parallel"), name="rmsnorm",
    )(x, g.reshape(1, D))


def _add_rmsnorm_body(x_ref, y_ref, g_ref, s_ref, o_ref):
    x = x_ref[...] + y_ref[...]
    s_ref[...] = x
    y = x * lax.rsqrt(jnp.mean(x * x, axis=-1, keepdims=True) + EPS)
    o_ref[...] = (y * g_ref[...]).astype(o_ref.dtype)


def add_rmsnorm(x, y, g, out_dtype=BF16):
    M, D = x.shape
    bm = _blk(M, 256)
    row = pl.BlockSpec((bm, D), lambda i: (i, 0))
    return pl.pallas_call(
        _add_rmsnorm_body, grid=(M // bm,),
        in_specs=[row, row, pl.BlockSpec((1, D), lambda i: (0, 0))],
        out_specs=[row, row],
        out_shape=[jax.ShapeDtypeStruct((M, D), F32), jax.ShapeDtypeStruct((M, D), out_dtype)],
        compiler_params=_cparams("parallel"), name="add_rmsnorm",
    )(x, y, g.reshape(1, D))


def _mm_body(x_ref, w_ref, o_ref):
    o_ref[...] = _dot(x_ref[...], w_ref[...]).astype(o_ref.dtype)


def _mm_res_body(x_ref, w_ref, r_ref, o_ref):
    o_ref[...] = (r_ref[...] + _dot(x_ref[...], w_ref[...])).astype(o_ref.dtype)


def matmul(x, w, res=None, out_dtype=F32, bm=1024, bn=1024):
    M, K = x.shape
    N = w.shape[1]
    bm, bn = _blk(M, bm), _blk(N, bn)
    in_specs = [pl.BlockSpec((bm, K), lambda i, j: (i, 0)), pl.BlockSpec((K, bn), lambda i, j: (0, j))]
    args = [x, w]
    body = _mm_body
    if res is not None:
        in_specs.append(pl.BlockSpec((bm, bn), lambda i, j: (i, j)))
        args.append(res)
        body = _mm_res_body
    return pl.pallas_call(
        body, grid=(M // bm, N // bn), in_specs=in_specs,
        out_specs=pl.BlockSpec((bm, bn), lambda i, j: (i, j)),
        out_shape=jax.ShapeDtypeStruct((M, N), out_dtype),
        compiler_params=_cparams("parallel", "parallel"), name="matmul",
    )(*args)


def _ffn_act(a, u):
    return (0.5 * a * (1.0 + lax.erf(a * (2.0 ** -0.5))) * u).astype(BF16)


def _ffn_seq_body(h_ref, wg_ref, wu_ref, wd_ref, cw_ref, cb_ref, o_ref, fst_ref, tail_ref, *, bm, blocks_per_seq):
    i = pl.program_id(0)
    f = pl.program_id(1)
    h = h_ref[...]
    g = _dot(h, wg_ref[...])
    u = _dot(h, wu_ref[...])

    @pl.when(i % blocks_per_seq == 0)
    def _():
        tail_ref[f] = jnp.zeros(tail_ref.shape[1:], F32)

    prev = tail_ref[f]
    p2, p1 = prev[0:1, :], prev[1:2, :]
    row = lax.broadcasted_iota(jnp.int32, g.shape, 0)
    s1 = jnp.where(row == 0, p1, pltpu.roll(g, 1, 0))
    s2 = jnp.where(row == 0, p2, jnp.where(row == 1, p1, pltpu.roll(g, 2, 0)))
    cw = cw_ref[...]
    a = s2 * cw[0:1, :] + s1 * cw[1:2, :] + g * cw[2:3, :] + cb_ref[...]
    last2 = g[bm - 2:bm, :]
    tail_ref[f, 0:2, :] = last2
    fst_ref[0, f] = last2
    d = _dot(_ffn_act(a, u), wd_ref[...])

    @pl.when(f == 0)
    def _():
        o_ref[...] = d

    @pl.when(f != 0)
    def _():
        o_ref[...] += d


def ffn_seq(h, wg, wu, wd, cw, cb, nseq, bm=512, bf=256):
    M, D = h.shape
    F = wg.shape[1]
    T = M // nseq
    bm = _blk(T, bm)
    nf = F // bf
    bps = T // bm
    body = functools.partial(_ffn_seq_body, bm=bm, blocks_per_seq=bps)
    y, fst = pl.pallas_call(
        body, grid=(M // bm, nf),
        in_specs=[pl.BlockSpec((bm, D), lambda i, f: (i, 0)),
                  pl.BlockSpec((D, bf), lambda i, f: (0, f)),
                  pl.BlockSpec((D, bf), lambda i, f: (0, f)),
                  pl.BlockSpec((bf, D), lambda i, f: (f, 0)),
                  pl.BlockSpec((FFN_CONV, bf), lambda i, f: (0, f)),
                  pl.BlockSpec((1, bf), lambda i, f: (0, f))],
        out_specs=[pl.BlockSpec((bm, D), lambda i, f: (i, 0)),
                   pl.BlockSpec((1, nf, 2, bf), lambda i, f: (i // bps, 0, 0, 0))],
        out_shape=[jax.ShapeDtypeStruct((M, D), F32), jax.ShapeDtypeStruct((nseq, nf, 2, bf), F32)],
        scratch_shapes=[pltpu.VMEM((nf, 8, bf), F32)],
        compiler_params=_cparams("arbitrary", "arbitrary"), name="ffn_seq",
    )(h, wg, wu, wd, cw, cb.reshape(1, F))
    return y, fst.transpose(0, 2, 1, 3).reshape(nseq, 2, F)


def _ffn_dec_body(h_ref, wg_ref, wu_ref, wd_ref, cw_ref, cb_ref, s0_ref, s1_ref, o_ref, g_ref):
    f = pl.program_id(0)
    h = h_ref[...]
    g = _dot(h, wg_ref[...])
    u = _dot(h, wu_ref[...])
    g_ref[...] = g
    cw = cw_ref[...]
    a = s0_ref[...] * cw[0:1, :] + s1_ref[...] * cw[1:2, :] + g * cw[2:3, :] + cb_ref[...]
    d = _dot(_ffn_act(a, u), wd_ref[...])

    @pl.when(f == 0)
    def _():
        o_ref[...] = d

    @pl.when(f != 0)
    def _():
        o_ref[...] += d


def ffn_dec(h, wg, wu, wd, cw, cb, st, bf=256):
    B, D = h.shape
    F = wg.shape[1]
    col = pl.BlockSpec((B, bf), lambda f: (0, f))
    return pl.pallas_call(
        _ffn_dec_body, grid=(F // bf,),
        in_specs=[pl.BlockSpec((B, D), lambda f: (0, 0)),
                  pl.BlockSpec((D, bf), lambda f: (0, f)),
                  pl.BlockSpec((D, bf), lambda f: (0, f)),
                  pl.BlockSpec((bf, D), lambda f: (f, 0)),
                  pl.BlockSpec((FFN_CONV, bf), lambda f: (0, f)),
                  pl.BlockSpec((1, bf), lambda f: (0, f)), col, col],
        out_specs=[pl.BlockSpec((B, D), lambda f: (0, 0)), col],
        out_shape=[jax.ShapeDtypeStruct((B, D), F32), jax.ShapeDtypeStruct((B, F), F32)],
        compiler_params=_cparams("arbitrary"), name="ffn_dec",
    )(h, wg, wu, wd, cw, cb.reshape(1, F), st[:, 0, :], st[:, 1, :])


def _log_gamma(h):
    return math.log(1.0 - 2.0 ** (-5.0 - h))


def _rope_tables(pos, half):
    inv = ROPE_THETA ** (-jnp.arange(half, dtype=F32) / half)
    ang = pos.astype(F32)[:, None] * inv[None, :]
    cos, sin = jnp.cos(ang), jnp.sin(ang)
    return jnp.concatenate([cos, cos], axis=-1), jnp.concatenate([-sin, sin], axis=-1)


def _rope(x, cosf, sins):
    return x * cosf + pltpu.roll(x, x.shape[-1] // 2, 1) * sins


def _ret_seq_body(q_ref, k_ref, v_ref, g_ref, cos_ref, sin_ref, o_ref, so_ref, s_ref, *, C, dk, dv):
    j = pl.program_id(1)

    @pl.when(j == 0)
    def _():
        s_ref[...] = jnp.zeros(s_ref.shape, F32)

    cosf, sins = cos_ref[...], sin_ref[...]
    ri = lax.broadcasted_iota(jnp.int32, (C, C), 0)
    ci = lax.broadcasted_iota(jnp.int32, (C, C), 1)
    idx = lax.broadcasted_iota(jnp.int32, (C, 1), 0).astype(F32)
    for h in range(RET_HEADS):
        lg = _log_gamma(h)
        q = _rope(q_ref[:, h * dk:(h + 1) * dk].astype(F32), cosf, sins)
        k = _rope(k_ref[:, h * dk:(h + 1) * dk].astype(F32), cosf, sins) * (dk ** -0.5)
        v = v_ref[:, h * dv:(h + 1) * dv]
        dmask = jnp.exp(jnp.where(ri >= ci, (ri - ci).astype(F32) * lg, NEG_BIG))
        scores = _dot_nt(q.astype(BF16), k.astype(BF16)) * dmask
        S = s_ref[h]
        qd = q * jnp.exp((idx + 1.0) * lg)
        o = _dot(scores.astype(BF16), v) + _dot(qd.astype(BF16), S.astype(BF16))
        kd = k * jnp.exp((C - 1.0 - idx) * lg)
        s_ref[h] = S * math.exp(C * lg) + _dot_tn(kd.astype(BF16), v)
        o = o * lax.rsqrt(jnp.mean(o * o, axis=-1, keepdims=True) + EPS)
        o_ref[:, h * dv:(h + 1) * dv] = (o * _silu(g_ref[:, h * dv:(h + 1) * dv].astype(F32))).astype(o_ref.dtype)

    @pl.when(j == pl.num_programs(1) - 1)
    def _():
        so_ref[0] = s_ref[...]


def retention_seq(z, B, T, dk, dv, C=256):
    C = _blk(T, C)
    nc = T // C
    qw, vw = RET_HEADS * dk, RET_HEADS * dv
    cosf, sins = _rope_tables(jnp.arange(T), dk // 2)
    body = functools.partial(_ret_seq_body, C=C, dk=dk, dv=dv)
    rows = lambda b, j: b * nc + j
    return pl.pallas_call(
        body, grid=(B, nc),
        in_specs=[pl.BlockSpec((C, qw), lambda b, j: (rows(b, j), 0)),
                  pl.BlockSpec((C, qw), lambda b, j: (rows(b, j), 1)),
                  pl.BlockSpec((C, vw), lambda b, j: (rows(b, j), (2 * qw) // vw)),
                  pl.BlockSpec((C, vw), lambda b, j: (rows(b, j), (2 * qw) // vw + 1)),
                  pl.BlockSpec((C, dk), lambda b, j: (j, 0)),
                  pl.BlockSpec((C, dk), lambda b, j: (j, 0))],
        out_specs=[pl.BlockSpec((C, vw), lambda b, j: (rows(b, j), 0)),
                   pl.BlockSpec((1, RET_HEADS, dk, dv), lambda b, j: (b, 0, 0, 0))],
        out_shape=[jax.ShapeDtypeStruct((B * T, vw), BF16), jax.ShapeDtypeStruct((B, RET_HEADS, dk, dv), F32)],
        scratch_shapes=[pltpu.VMEM((RET_HEADS, dk, dv), F32)],
        compiler_params=_cparams("arbitrary", "arbitrary"), name="retention_seq",
    )(z, z, z, z, cosf, sins)


def _row0(x, rows=8):
    r = lax.broadcasted_iota(jnp.int32, (rows, x.shape[1]), 0)
    return jnp.where(r == 0, jnp.broadcast_to(x, (rows, x.shape[1])), 0.0)


def _ret_dec_body(q_ref, k_ref, v_ref, g_ref, cos_ref, sin_ref, s_ref, o_ref, so_ref, *, bb, dk, dv):
    cosf, sins = cos_ref[...], sin_ref[...]
    for h in range(RET_HEADS):
        gamma = math.exp(_log_gamma(h))
        q = _rope(q_ref[:, h * dk:(h + 1) * dk].astype(F32), cosf, sins)
        k = _rope(k_ref[:, h * dk:(h + 1) * dk].astype(F32), cosf, sins) * (dk ** -0.5)
        v = v_ref[:, h * dv:(h + 1) * dv].astype(F32)
        gate = _silu(g_ref[:, h * dv:(h + 1) * dv].astype(F32))
        for b in range(bb):
            Sn = s_ref[b, h] * gamma + _dot_tn(_row0(k[b:b + 1]).astype(BF16), _row0(v[b:b + 1]).astype(BF16))
            so_ref[b, h] = Sn
            o = _dot(_row0(q[b:b + 1]).astype(BF16), Sn.astype(BF16))[0:1]
            o = o * lax.rsqrt(jnp.mean(o * o, axis=-1, keepdims=True) + EPS)
            o_ref[b:b + 1, h * dv:(h + 1) * dv] = (o * gate[b:b + 1]).astype(o_ref.dtype)


def retention_dec(z, state, dk, dv, bb=8):
    B = z.shape[0]
    bb = _blk(B, bb)
    qw, vw = RET_HEADS * dk, RET_HEADS * dv
    cosf, sins = _rope_tables(jnp.full((1,), PAST_LEN), dk // 2)
    body = functools.partial(_ret_dec_body, bb=bb, dk=dk, dv=dv)
    st = pl.BlockSpec((bb, RET_HEADS, dk, dv), lambda i: (i, 0, 0, 0))
    return pl.pallas_call(
        body, grid=(B // bb,),
        in_specs=[pl.BlockSpec((bb, qw), lambda i: (i, 0)),
                  pl.BlockSpec((bb, qw), lambda i: (i, 1)),
                  pl.BlockSpec((bb, vw), lambda i: (i, (2 * qw) // vw)),
                  pl.BlockSpec((bb, vw), lambda i: (i, (2 * qw) // vw + 1)),
                  pl.BlockSpec((1, dk), lambda i: (0, 0)),
                  pl.BlockSpec((1, dk), lambda i: (0, 0)), st],
        out_specs=[pl.BlockSpec((bb, vw), lambda i: (i, 0)), st],
        out_shape=[jax.ShapeDtypeStruct((B, vw), BF16), jax.ShapeDtypeStruct(state.shape, F32)],
        compiler_params=_cparams("parallel"), name="retention_dec",
    )(z, z, z, z, cosf, sins, state)


def _pool_seq_body(u_ref, w_ref, ls_ref, o_ref, ext_ref, *, R, gw):
    j = pl.program_id(1)

    @pl.when(j == 0)
    def _():
        ext_ref[0:16, :] = jnp.zeros((16, ext_ref.shape[1]), F32)

    ext_ref[16:16 + R, :] = u_ref[...].astype(F32)
    pos = j * R + lax.broadcasted_iota(jnp.int32, (R, 1), 0)
    for gi, win in enumerate(POOL_WINDOWS):
        sl = slice(gi * gw, (gi + 1) * gw)
        cur = ext_ref[16:16 + R, sl]
        acc = cur
        for s in range(1, win):
            acc = acc + ext_ref[16 - s:16 - s + R, sl]
        cnt = jnp.minimum(pos + 1, win).astype(F32)
        d = acc / cnt - cur
        o_ref[:, sl] = (_dot(d.astype(BF16), w_ref[gi]) * ls_ref[:, sl]).astype(o_ref.dtype)
    ext_ref[0:16, :] = ext_ref[R:R + 16, :]


def pool_seq(z, w_pool, ls, B, T, col_block, R=256):
    R = _blk(T, R)
    nr = T // R
    gw = w_pool.shape[-1]
    W = gw * len(POOL_WINDOWS)
    body = functools.partial(_pool_seq_body, R=R, gw=gw)
    return pl.pallas_call(
        body, grid=(B, nr),
        in_specs=[pl.BlockSpec((R, W), lambda b, j: (b * nr + j, col_block)),
                  pl.BlockSpec(w_pool.shape, lambda b, j: (0, 0, 0)),
                  pl.BlockSpec((1, W), lambda b, j: (0, 0))],
        out_specs=pl.BlockSpec((R, W), lambda b, j: (b * nr + j, 0)),
        out_shape=jax.ShapeDtypeStruct((B * T, W), BF16),
        scratch_shapes=[pltpu.VMEM((16 + R, W), F32)],
        compiler_params=_cparams("arbitrary", "arbitrary"), name="pool_seq",
    )(z, w_pool, ls.reshape(1, W))


def _pool_dec_body(u_ref, st_ref, w_ref, ls_ref, o_ref, *, gw, W):
    for gi, win in enumerate(POOL_WINDOWS):
        sl = slice(gi * gw, (gi + 1) * gw)
        cur = u_ref[:, sl].astype(F32)
        acc = cur
        for s in range(1, win):
            r = POOL_BUF - s
            acc = acc + st_ref[:, r * W + gi * gw:r * W + (gi + 1) * gw]
        cnt = float(min(PAST_LEN + 1, win))
        d = acc / cnt - cur
        o_ref[:, sl] = (_dot(d.astype(BF16), w_ref[gi]) * ls_ref[:, sl]).astype(o_ref.dtype)


def pool_dec(z, state, w_pool, ls, col_block, bb=64):
    B = z.shape[0]
    bb = _blk(B, bb)
    gw = w_pool.shape[-1]
    W = gw * len(POOL_WINDOWS)
    body = functools.partial(_pool_dec_body, gw=gw, W=W)
    return pl.pallas_call(
        body, grid=(B // bb,),
        in_specs=[pl.BlockSpec((bb, W), lambda i: (i, col_block)),
                  pl.BlockSpec((bb, POOL_BUF * W), lambda i: (i, 0)),
                  pl.BlockSpec(w_pool.shape, lambda i: (0, 0, 0)),
                  pl.BlockSpec((1, W), lambda i: (0, 0))],
        out_specs=pl.BlockSpec((bb, W), lambda i: (i, 0)),
        out_shape=jax.ShapeDtypeStruct((B, W), BF16),
        compiler_params=_cparams("parallel"), name="pool_dec",
    )(z, state.reshape(B, POOL_BUF * W), w_pool, ls.reshape(1, W))


def _l2norm(x):
    return x * lax.rsqrt(jnp.sum(x * x, axis=-1, keepdims=True) + EPS)


def _softplus(x):
    return jnp.maximum(x, 0.0) + jnp.log1p(jnp.exp(-jnp.abs(x)))


def _sigmoid(x):
    return 1.0 / (1.0 + jnp.exp(-x))


def _solve_unit_lower(L, R):
    c = L.shape[0]
    nb = c // GDN_SUB
    blocks = []
    for a in range(nb):
        r0 = a * GDN_SUB
        Rb = R[r0:r0 + GDN_SUB, :]
        if a:
            Xp = jnp.concatenate(blocks + [jnp.zeros((c - r0, R.shape[1]), F32)], axis=0)
            Rb = Rb - _dot(L[r0:r0 + GDN_SUB, :].astype(BF16), Xp.astype(BF16))
        Lb = L[r0:r0 + GDN_SUB, r0:r0 + GDN_SUB]
        for jj in range(GDN_SUB - 1):
            Rb = Rb - Lb[:, jj:jj + 1] * Rb[jj:jj + 1, :]
        blocks.append(Rb)
    return jnp.concatenate(blocks, axis=0)


def _gdn_seq_body(q_ref, k_ref, v_ref, z_ref, wq_ref, wk_ref, wv_ref, ba_ref, al_ref, dt_ref, nw_ref,
                  o_ref, so_ref, s_ref, eq_ref, ek_ref, ev_ref, cq_ref, ck_ref, cv_ref, *, R, dk, dv):
    r = pl.program_id(2)
    c = GDN_CHUNK
    hb = GDN_HB

    @pl.when(r == 0)
    def _():
        s_ref[...] = jnp.zeros(s_ref.shape, F32)
        for e in (eq_ref, ek_ref, ev_ref):
            e[0:8, :] = jnp.zeros((8, e.shape[1]), F32)

    for x_ref, w_ref, e_ref, c_ref in ((q_ref, wq_ref, eq_ref, cq_ref), (k_ref, wk_ref, ek_ref, ck_ref),
                                       (v_ref, wv_ref, ev_ref, cv_ref)):
        e_ref[8:8 + R, :] = x_ref[...].astype(F32)
        w = w_ref[...]
        acc = e_ref[5:5 + R, :] * w[0:1, :]
        for i in range(1, GDN_CONV):
            acc = acc + e_ref[5 + i:5 + i + R, :] * w[i:i + 1, :]
        c_ref[...] = _silu(acc)
        e_ref[0:8, :] = e_ref[R:R + 8, :]

    ri = lax.broadcasted_iota(jnp.int32, (c, c), 0)
    ci = lax.broadcasted_iota(jnp.int32, (c, c), 1)
    tril = (ri >= ci).astype(F32)
    triu = (ri <= ci).astype(F32)
    neg_a = -jnp.exp(al_ref[0])
    dtb = dt_ref[0]
    nw = nw_ref[...]

    def chunk(ic, carry):
        r0 = pl.multiple_of(ic * c, c)
        rows = pl.ds(r0, c)
        x = ba_ref[rows, :]
        beta_all = _sigmoid(x)
        g_all = neg_a * _softplus(x + dtb)
        b_col = jnp.dot(tril, g_all, precision=lax.Precision.HIGHEST, preferred_element_type=F32)
        b_row = lax.dot_general(g_all, triu, (((0,), (0,)), ((), ())), precision=lax.Precision.HIGHEST,
                                preferred_element_type=F32)
        for hh in range(hb):
            hs = slice(hh * dk, (hh + 1) * dk)
            q = _l2norm(cq_ref[rows, hs]) * (dk ** -0.5)
            k = _l2norm(ck_ref[rows, hs])
            v = cv_ref[rows, hh * dv:(hh + 1) * dv]
            bc = b_col[:, hb + hh:hb + hh + 1]
            br = b_row[hb + hh:hb + hh + 1, :]
            beta = beta_all[:, hh:hh + 1]
            diff = bc - br
            decay = jnp.exp(jnp.where(ri >= ci, diff, NEG_BIG))
            kb = k * beta
            kbf = k.astype(BF16)
            L = jnp.where(ri > ci, _dot_nt(kb.astype(BF16), kbf) * decay, 0.0)
            eb = jnp.exp(bc)
            rhs = jnp.concatenate([v * beta, kb * eb], axis=1)
            uw = _solve_unit_lower(L, rhs)
            u, w = uw[:, :dv], uw[:, dv:]
            attn = _dot_nt(q.astype(BF16), kbf) * decay
            S = s_ref[hh]
            Sb = S.astype(BF16)
            v_new = u - _dot(w.astype(BF16), Sb)
            vnb = v_new.astype(BF16)
            o = _dot((q * eb).astype(BF16), Sb) + _dot(attn.astype(BF16), vnb)
            bl = bc[c - 1:c, :]
            s_ref[hh] = S * jnp.exp(bl) + _dot_tn((k * jnp.exp(bl - bc)).astype(BF16), vnb)
            o = o * lax.rsqrt(jnp.mean(o * o, axis=-1, keepdims=True) + EPS) * nw
            gate = _silu(z_ref[rows, hh * dv:(hh + 1) * dv].astype(F32))
            o_ref[rows, hh * dv:(hh + 1) * dv] = (o * gate).astype(o_ref.dtype)
        return carry

    lax.fori_loop(0, R // c, chunk, 0)

    @pl.when(r == pl.num_programs(2) - 1)
    def _():
        so_ref[0] = s_ref[...]


def _group_lanes(t):
    hb, HG = GDN_HB, GDN_HEADS // GDN_HB
    t = t.reshape(t.shape[:-1] + (HG, hb))
    t = jnp.pad(t, [(0, 0)] * (t.ndim - 1) + [(hb, LANES - 2 * hb)])
    return t.reshape(t.shape[:-2] + (HG * LANES,))


def gdn_seq(z, zba, w_gconv, a_log, dt_bias, norm_w, B, T, dk, dv, col0, R=256):
    H, hb = GDN_HEADS, GDN_HB
    HG = H // hb
    R = _blk(T, R)
    nr = T // R
    M = B * T
    bw = hb * dk
    cb0 = col0 // bw
    per = (H * dk) // bw
    body = functools.partial(_gdn_seq_body, R=R, dk=dk, dv=dv)
    rows = lambda b, g, r: b * nr + r
    zspec = lambda off: pl.BlockSpec((R, bw), lambda b, g, r: (rows(b, g, r), cb0 + off + g))
    wspec = lambda off: pl.BlockSpec((GDN_CONV, bw), lambda b, g, r: (0, off + g))
    pspec = pl.BlockSpec((1, 1, LANES), lambda b, g, r: (g, 0, 0))
    return pl.pallas_call(
        body, grid=(B, HG, nr),
        in_specs=[zspec(0), zspec(per), zspec(2 * per), zspec(3 * per),
                  wspec(0), wspec(per), wspec(2 * per),
                  pl.BlockSpec((R, LANES), lambda b, g, r: (rows(b, g, r), g)), pspec, pspec,
                  pl.BlockSpec((1, dv), lambda b, g, r: (0, 0))],
        out_specs=[pl.BlockSpec((R, bw), lambda b, g, r: (rows(b, g, r), g)),
                   pl.BlockSpec((1, hb, dk, dv), lambda b, g, r: (b, g, 0, 0))],
        out_shape=[jax.ShapeDtypeStruct((M, H * dv), BF16), jax.ShapeDtypeStruct((B, H, dk, dv), F32)],
        scratch_shapes=[pltpu.VMEM((hb, dk, dv), F32)] + [pltpu.VMEM((8 + R, bw), F32)] * 3
                       + [pltpu.VMEM((R, bw), F32)] * 3,
        compiler_params=_cparams("arbitrary", "arbitrary", "arbitrary"), name="gdn_seq",
    )(z, z, z, z, w_gconv, w_gconv, w_gconv, zba,
      _group_lanes(a_log).reshape(HG, 1, LANES), _group_lanes(dt_bias).reshape(HG, 1, LANES),
      norm_w.reshape(1, dv))


def _gdn_dec_body(xq_ref, xk_ref, xv_ref, z_ref, cs_ref, w_ref, b_ref, a_ref, al_ref, dt_ref, nw_ref, s_ref,
                  o_ref, so_ref, cu_ref, *, bb, dk, dv, CH):
    H = GDN_HEADS
    GW = CH // 3
    for gi, x_ref in enumerate((xq_ref, xk_ref, xv_ref)):
        cs = slice(gi * GW, (gi + 1) * GW)
        acc = x_ref[...].astype(F32) * w_ref[GDN_CONV - 1:GDN_CONV, cs]
        for i in range(GDN_CONV - 1):
            acc = acc + cs_ref[:, i * CH + gi * GW:i * CH + (gi + 1) * GW] * w_ref[i:i + 1, cs]
        cu_ref[:, cs] = _silu(acc)
    beta = _sigmoid(b_ref[...])
    eg = jnp.exp(-jnp.exp(al_ref[...]) * _softplus(a_ref[...] + dt_ref[...]))
    nw = nw_ref[...]
    qo, ko, vo = 0, H * dk, 2 * H * dk
    for h in range(H):
        q = _l2norm(cu_ref[:, qo + h * dk:qo + (h + 1) * dk]) * (dk ** -0.5)
        k = _l2norm(cu_ref[:, ko + h * dk:ko + (h + 1) * dk])
        v = cu_ref[:, vo + h * dv:vo + (h + 1) * dv]
        qk = jnp.sum(q * k, axis=-1, keepdims=True)
        gate = _silu(z_ref[:, h * dv:(h + 1) * dv].astype(F32))
        for b in range(bb):
            S = s_ref[b, h]
            kq = jnp.concatenate([k[b:b + 1], q[b:b + 1], jnp.zeros((6, dk), F32)], axis=0)
            ks_qs = _dot(kq.astype(BF16), S.astype(BF16))
            e = eg[b:b + 1, h:h + 1]
            v_new = beta[b:b + 1, h:h + 1] * (v[b:b + 1] - e * ks_qs[0:1])
            o = e * ks_qs[1:2] + qk[b:b + 1] * v_new
            so_ref[b, h] = S * e + _dot_tn(_row0(k[b:b + 1]).astype(BF16), _row0(v_new).astype(BF16))
            o = o * lax.rsqrt(jnp.mean(o * o, axis=-1, keepdims=True) + EPS) * nw
            o_ref[b:b + 1, h * dv:(h + 1) * dv] = (o * gate[b:b + 1]).astype(o_ref.dtype)


def gdn_dec(z, zb, za, conv_state, w_gconv, a_log, dt_bias, norm_w, state, dk, dv, col0, bb=8):
    B = z.shape[0]
    H = GDN_HEADS
    bb = _blk(B, bb)
    CH = 3 * H * dk
    GW = H * dk
    cb0 = col0 // GW
    body = functools.partial(_gdn_dec_body, bb=bb, dk=dk, dv=dv, CH=CH)
    st = pl.BlockSpec((bb, H, dk, dv), lambda i: (i, 0, 0, 0))
    hrow = pl.BlockSpec((bb, H), lambda i: (i, 0))
    hpar = pl.BlockSpec((1, H), lambda i: (0, 0))
    zspec = lambda off: pl.BlockSpec((bb, GW), lambda i: (i, cb0 + off))
    return pl.pallas_call(
        body, grid=(B // bb,),
        in_specs=[zspec(0), zspec(1), zspec(2), zspec(3),
                  pl.BlockSpec((bb, (GDN_CONV - 1) * CH), lambda i: (i, 0)),
                  pl.BlockSpec((GDN_CONV, CH), lambda i: (0, 0)),
                  hrow, hrow, hpar, hpar,
                  pl.BlockSpec((1, dv), lambda i: (0, 0)), st],
        out_specs=[pl.BlockSpec((bb, H * dv), lambda i: (i, 0)), st],
        out_shape=[jax.ShapeDtypeStruct((B, H * dv), BF16), jax.ShapeDtypeStruct(state.shape, F32)],
        scratch_shapes=[pltpu.VMEM((bb, CH), F32)],
        compiler_params=_cparams("parallel"), name="gdn_dec",
    )(z, z, z, z, conv_state.reshape(B, (GDN_CONV - 1) * CH), w_gconv, zb, za,
      a_log.reshape(1, H), dt_bias.reshape(1, H), norm_w.reshape(1, dv), state)


def _softmax_rows(s):
    m = jnp.max(s, axis=-1, keepdims=True)
    e = jnp.exp(s - m)
    return e / jnp.sum(e, axis=-1, keepdims=True)


def _xattn_seq_body(q_ref, k_ref, v_ref, o_ref, *, hd):
    for h in range(X_HEADS):
        sl = slice(h * hd, (h + 1) * hd)
        s = _dot_nt(q_ref[:, sl], k_ref[0, :, sl].astype(BF16)) * (hd ** -0.5)
        a = _softmax_rows(s)
        o_ref[:, sl] = _dot(a.astype(BF16), v_ref[0, :, sl].astype(BF16)).astype(o_ref.dtype)


def xattn_seq(q, mk, mv, B, T, bq=512):
    W = q.shape[1]
    hd = W // X_HEADS
    bq = _blk(T, bq)
    nq = T // bq
    mem = pl.BlockSpec((1,) + mk.shape[1:], lambda b, j: (b, 0, 0))
    return pl.pallas_call(
        functools.partial(_xattn_seq_body, hd=hd), grid=(B, nq),
        in_specs=[pl.BlockSpec((bq, W), lambda b, j: (b * nq + j, 0)), mem, mem],
        out_specs=pl.BlockSpec((bq, W), lambda b, j: (b * nq + j, 0)),
        out_shape=jax.ShapeDtypeStruct((B * T, W), BF16),
        compiler_params=_cparams("parallel", "parallel"), name="xattn_seq",
    )(q, mk, mv)


def _xattn_dec_body(q_ref, k_ref, v_ref, o_ref, *, bb, hd):
    q = q_ref[...]
    for b in range(bb):
        for h in range(X_HEADS):
            sl = slice(h * hd, (h + 1) * hd)
            q8 = jnp.broadcast_to(q[b:b + 1, sl], (8, hd))
            s = _dot_nt(q8, k_ref[b, :, sl].astype(BF16)) * (hd ** -0.5)
            a = _softmax_rows(s)
            o = _dot(a.astype(BF16), v_ref[b, :, sl].astype(BF16))
            o_ref[b:b + 1, sl] = o[0:1].astype(o_ref.dtype)


def xattn_dec(q, mk, mv, bb=8):
    B, W = q.shape
    hd = W // X_HEADS
    bb = _blk(B, bb)
    mem = pl.BlockSpec((bb,) + mk.shape[1:], lambda i: (i, 0, 0))
    return pl.pallas_call(
        functools.partial(_xattn_dec_body, bb=bb, hd=hd), grid=(B // bb,),
        in_specs=[pl.BlockSpec((bb, W), lambda i: (i, 0)), mem, mem],
        out_specs=pl.BlockSpec((bb, W), lambda i: (i, 0)),
        out_shape=jax.ShapeDtypeStruct((B, W), BF16),
        compiler_params=_cparams("parallel"), name="xattn_dec",
    )(q, mk, mv)


def _run_trunk(x, nseq, T, states, mem_k, mem_v, W, depth):
    D = x.shape[1]
    dk_r, dv_r = D // 32, D // 16
    dk_g = dv_g = D // 32
    n_main = W["w_in"][0].shape[1]
    pool_w = W["w_pool"].shape[-1] * len(POOL_WINDOWS)
    off_pu = 2 * RET_HEADS * dk_r + 2 * RET_HEADS * dv_r
    off_c = off_pu + pool_w
    ch = 3 * GDN_HEADS * dk_g
    outs = [[] for _ in range(5)]
    hn = rmsnorm(x, W["norm_mix"][0])
    for l in range(depth):
        z = matmul(hn, W["w_in"][l], out_dtype=BF16)
        zba = matmul(hn, W["w_ba"][l], out_dtype=F32)
        if states is None:
            ro, s_ret = retention_seq(z, nseq, T, dk_r, dv_r)
            po = pool_seq(z, W["w_pool"][l], W["ls_pool"][l], nseq, T, off_pu // pool_w)
            co, s_gdn = gdn_seq(z, zba, W["w_gconv"][l], W["gdn_a_log"][l], W["gdn_dt_bias"][l],
                                W["gdn_norm"][l], nseq, T, dk_g, dv_g, off_c)
            z3 = z.reshape(nseq, T, n_main)
            s_pool = z3[:, T - POOL_BUF:, off_pu:off_pu + pool_w].astype(F32)
            s_gconv = z3[:, T - (GDN_CONV - 1):, off_c:off_c + ch].astype(F32)
        else:
            st_ret, st_pool, st_gdn, st_gconv, st_fconv = (s[l] for s in states)
            zg = zba.reshape(zba.shape[0], GDN_HEADS // GDN_HB, LANES)
            zb = zg[:, :, :GDN_HB].reshape(-1, GDN_HEADS)
            za = zg[:, :, GDN_HB:2 * GDN_HB].reshape(-1, GDN_HEADS)
            ro, s_ret = retention_dec(z, st_ret, dk_r, dv_r)
            po = pool_dec(z, st_pool, W["w_pool"][l], W["ls_pool"][l], off_pu // pool_w)
            co, s_gdn = gdn_dec(z, zb, za, st_gconv, W["w_gconv"][l], W["gdn_a_log"][l], W["gdn_dt_bias"][l],
                                W["gdn_norm"][l], st_gdn, dk_g, dv_g, off_c)
            s_pool = jnp.concatenate([st_pool[:, 1:], z[:, None, off_pu:off_pu + pool_w].astype(F32)], axis=1)
            s_gconv = jnp.concatenate([st_gconv[:, 1:], z[:, None, off_c:off_c + ch].astype(F32)], axis=1)
        mix = jnp.concatenate([ro, po, co], axis=-1)
        x = matmul(mix, W["w_out"][l], res=x)
        hx = rmsnorm(x, W["norm_x"][l])
        q = matmul(hx, W["w_xq"][l], out_dtype=BF16)
        mk = mem_k[l].reshape(mem_k.shape[1], mem_k.shape[2], -1)
        mv = mem_v[l].reshape(mem_v.shape[1], mem_v.shape[2], -1)
        ao = xattn_seq(q, mk, mv, nseq, T) if states is None else xattn_dec(q, mk, mv)
        x = matmul(ao, W["w_xo"][l], res=x)
        hf = rmsnorm(x, W["norm_ffn"][l])
        if states is None:
            y, s_fconv = ffn_seq(hf, W["w_gate"][l], W["w_up"][l], W["w_down"][l], W["w_fconv"][l],
                                 W["b_fconv"][l], nseq)
        else:
            y, g_new = ffn_dec(hf, W["w_gate"][l], W["w_up"][l], W["w_down"][l], W["w_fconv"][l],
                               W["b_fconv"][l], st_fconv)
            s_fconv = jnp.concatenate([st_fconv[:, 1:], g_new[:, None, :]], axis=1)
        nxt = W["norm_mix"][l + 1] if l + 1 < depth else W["norm_f"]
        x, hn = add_rmsnorm(x, y, nxt, out_dtype=BF16 if l + 1 < depth else F32)
        for lst, s in zip(outs, (s_ret, s_pool, s_gdn, s_gconv, s_fconv)):
            lst.append(s)
    return hn, [jnp.stack(s) for s in outs]


def kernel(x_prompt, x_sample, state_ret, state_pool, state_gdn, state_gdn_conv, state_ffn_conv, cache_mem_k, cache_mem_v, mem_prompt, norm_mix, w_in, w_pool, ls_pool, w_gconv, gdn_a_log, gdn_dt_bias, gdn_norm, w_out, norm_x, norm_mem, w_xq, w_xk, w_xv, w_xo, norm_ffn, w_gate, w_up, w_fconv, b_fconv, w_down, norm_f):
    depth = w_in.shape[0]
    Bp, T, D = x_prompt.shape
    Bs = x_sample.shape[0]
    assert x_sample.shape[1] == 1 and D % (32 * LANES) == 0
    n_in = w_in.shape[2]
    n_main = n_in - 2 * GDN_HEADS
    hg = GDN_HEADS // GDN_HB
    w_b = w_in[:, :, n_main:n_main + GDN_HEADS].reshape(depth, D, hg, GDN_HB)
    w_a = w_in[:, :, n_main + GDN_HEADS:].reshape(depth, D, hg, GDN_HB)
    w_ba = jnp.pad(jnp.concatenate([w_b, w_a], axis=-1), ((0, 0), (0, 0), (0, 0), (0, LANES - 2 * GDN_HB)))
    w_ba = w_ba.reshape(depth, D, hg * LANES)
    W = dict(norm_mix=norm_mix, norm_x=norm_x, norm_ffn=norm_ffn, norm_f=norm_f,
             w_in=w_in[:, :, :n_main].astype(BF16), w_ba=w_ba.astype(BF16),
             w_pool=w_pool.astype(BF16), ls_pool=ls_pool, w_gconv=w_gconv,
             gdn_a_log=gdn_a_log, gdn_dt_bias=gdn_dt_bias, gdn_norm=gdn_norm,
             w_out=w_out.astype(BF16), w_xq=w_xq.astype(BF16), w_xo=w_xo.astype(BF16),
             w_gate=w_gate.astype(BF16), w_up=w_up.astype(BF16), w_down=w_down.astype(BF16),
             w_fconv=w_fconv, b_fconv=b_fconv)
    n_mem = mem_prompt.shape[1]
    memf = mem_prompt.reshape(Bp * n_mem, D)
    pk, pv = [], []
    for l in range(depth):
        mn = rmsnorm(memf, norm_mem[l])
        pk.append(matmul(mn, w_xk[l].astype(BF16)).reshape(Bp, n_mem, X_HEADS, -1))
        pv.append(matmul(mn, w_xv[l].astype(BF16)).reshape(Bp, n_mem, X_HEADS, -1))
    p_mem_k, p_mem_v = jnp.stack(pk), jnp.stack(pv)
    y_p, p_st = _run_trunk(x_prompt.reshape(Bp * T, D), Bp, T, None, p_mem_k, p_mem_v, W, depth)
    y_s, s_st = _run_trunk(x_sample.reshape(Bs, D), Bs, 1,
                           (state_ret, state_pool, state_gdn, state_gdn_conv, state_ffn_conv),
                           cache_mem_k, cache_mem_v, W, depth)
    return (y_p.reshape(Bp, T, D), y_s.reshape(Bs, 1, D), *p_st, p_mem_k, p_mem_v, *s_st)
```

```python
import functools
import math

import jax
import jax.numpy as jnp
from jax import lax
from jax.experimental import pallas as pl
from jax.experimental.pallas import tpu as pltpu

F32 = jnp.float32
BF16 = jnp.bfloat16

EPS = 1e-6
ROPE_THETA = 10000.0
RET_HEADS = 4
GDN_HEADS = 16
X_HEADS = 4
POOL_WINDOWS = (2, 4, 8, 16)
POOL_BUF = max(POOL_WINDOWS) - 1
GDN_CONV = 4
FFN_CONV = 3
PAST_LEN = 16384

V7X_VMEM_BYTES = 64 * 1024 * 1024
VMEM_LIMIT = V7X_VMEM_BYTES - 8 * 1024 * 1024
LANES = 128
GDN_CHUNK = 64
GDN_SUB = 16
GDN_HB = 16
GDN_SOLVE = 8
NEG_BIG = -1e30


def _cparams(*sem):
    return pltpu.CompilerParams(dimension_semantics=sem, vmem_limit_bytes=VMEM_LIMIT)


def _dot(a, b):
    return jnp.dot(a, b, preferred_element_type=F32)


def _dot_nt(a, b):
    return lax.dot_general(a, b, (((1,), (1,)), ((), ())), preferred_element_type=F32)


def _dot_tn(a, b):
    return lax.dot_general(a, b, (((0,), (0,)), ((), ())), preferred_element_type=F32)


def _silu(x):
    return (0.5 * x) * (1.0 + jnp.tanh(0.5 * x))


def _blk(n, want):
    b = min(n, want)
    while n % b:
        b //= 2
    return b


def _rmsnorm_body(x_ref, g_ref, o_ref):
    x = x_ref[...]
    y = x * lax.rsqrt(jnp.mean(x * x, axis=-1, keepdims=True) + EPS)
    o_ref[...] = (y * g_ref[...]).astype(o_ref.dtype)


def _gain_spec(l, D):
    return pl.BlockSpec((None, 1, D), lambda i: (l, 0, 0))


def rmsnorm(x, g, l, out_dtype=BF16):
    M, D = x.shape
    bm = _blk(M, 256)
    return pl.pallas_call(
        _rmsnorm_body, grid=(M // bm,),
        in_specs=[pl.BlockSpec((bm, D), lambda i: (i, 0)), _gain_spec(l, D)],
        out_specs=pl.BlockSpec((bm, D), lambda i: (i, 0)),
        out_shape=jax.ShapeDtypeStruct((M, D), out_dtype),
        compiler_params=_cparams("parallel"), name="rmsnorm",
    )(x, g.reshape(-1, 1, D))


def _add_rmsnorm_body(x_ref, y_ref, g_ref, s_ref, o_ref):
    x = x_ref[...] + y_ref[...]
    s_ref[...] = x
    y = x * lax.rsqrt(jnp.mean(x * x, axis=-1, keepdims=True) + EPS)
    o_ref[...] = (y * g_ref[...]).astype(o_ref.dtype)


def add_rmsnorm(x, y, g, l, out_dtype=BF16):
    M, D = x.shape
    bm = _blk(M, 256)
    row = pl.BlockSpec((bm, D), lambda i: (i, 0))
    return pl.pallas_call(
        _add_rmsnorm_body, grid=(M // bm,),
        in_specs=[row, row, _gain_spec(l, D)],
        out_specs=[row, row],
        out_shape=[jax.ShapeDtypeStruct((M, D), F32), jax.ShapeDtypeStruct((M, D), out_dtype)],
        compiler_params=_cparams("parallel"), name="add_rmsnorm",
    )(x, y, g.reshape(-1, 1, D))


def _mm_body(*refs, widths, has_res):
    xs, w_ref = refs[:len(widths)], refs[len(widths)]
    o_ref = refs[-1]
    acc = refs[len(widths) + 1][...] if has_res else None
    k0 = 0
    for x_ref, kw in zip(xs, widths):
        d = _dot(x_ref[...], w_ref[k0:k0 + kw, :])
        acc = d if acc is None else acc + d
        k0 += kw
    o_ref[...] = acc.astype(o_ref.dtype)


def matmul(xs, w, l, res=None, out_dtype=F32, bm=1024, bn=1024):
    xs = list(xs) if isinstance(xs, (list, tuple)) else [xs]
    M = xs[0].shape[0]
    widths = tuple(x.shape[1] for x in xs)
    K, N = w.shape[1:]
    assert sum(widths) == K
    bm, bn = _blk(M, bm), _blk(N, bn)
    in_specs = [pl.BlockSpec((bm, kw), lambda i, j: (i, 0)) for kw in widths]
    in_specs.append(pl.BlockSpec((None, K, bn), lambda i, j: (l, 0, j)))
    args = xs + [w]
    if res is not None:
        in_specs.append(pl.BlockSpec((bm, bn), lambda i, j: (i, j)))
        args.append(res)
    return pl.pallas_call(
        functools.partial(_mm_body, widths=widths, has_res=res is not None),
        grid=(M // bm, N // bn), in_specs=in_specs,
        out_specs=pl.BlockSpec((bm, bn), lambda i, j: (i, j)),
        out_shape=jax.ShapeDtypeStruct((M, N), out_dtype),
        compiler_params=_cparams("parallel", "parallel"), name="matmul",
    )(*args)


def _ffn_act(a, u):
    return (0.5 * a * (1.0 + lax.erf(a * (2.0 ** -0.5))) * u).astype(BF16)


def _ffn_seq_body(h_ref, wg_ref, wu_ref, wd_ref, cw_ref, cb_ref, o_ref, fst_ref, tail_ref, act_ref, *,
                  bm, blocks_per_seq, nf):
    i = pl.program_id(0)
    f = pl.program_id(1)
    fi = jnp.minimum(f, nf - 1)

    @pl.when(f == 0)
    def _():
        act_ref[...] = jnp.zeros(act_ref.shape, act_ref.dtype)
        o_ref[...] = jnp.zeros(o_ref.shape, o_ref.dtype)

    @pl.when(i % blocks_per_seq == 0)
    def _():
        tail_ref[fi] = jnp.zeros(tail_ref.shape[1:], F32)

    o_ref[...] += _dot(act_ref[...], wd_ref[...])
    h = h_ref[...]
    g = _dot(h, wg_ref[...])
    u = _dot(h, wu_ref[...])
    prev = tail_ref[fi]
    p2, p1 = prev[0:1, :], prev[1:2, :]
    row = lax.broadcasted_iota(jnp.int32, g.shape, 0)
    s1 = jnp.where(row == 0, p1, pltpu.roll(g, 1, 0))
    s2 = jnp.where(row == 0, p2, jnp.where(row == 1, p1, pltpu.roll(g, 2, 0)))
    cw = cw_ref[...]
    a = s2 * cw[0:1, :] + s1 * cw[1:2, :] + g * cw[2:3, :] + cb_ref[...]
    last2 = g[bm - 2:bm, :]
    tail_ref[fi, 0:2, :] = last2
    fst_ref[0, fi] = last2
    act_ref[...] = _ffn_act(a, u)


def ffn_seq(h, wg, wu, wd, cw, cb, l, nseq, bm=512, bf=256):
    M, D = h.shape
    F = wg.shape[2]
    T = M // nseq
    bm = _blk(T, bm)
    nf = F // bf
    bps = T // bm
    body = functools.partial(_ffn_seq_body, bm=bm, blocks_per_seq=bps, nf=nf)
    cur = lambda f: jnp.minimum(f, nf - 1)
    y, fst = pl.pallas_call(
        body, grid=(M // bm, nf + 1),
        in_specs=[pl.BlockSpec((bm, D), lambda i, f: (i, 0)),
                  pl.BlockSpec((None, D, bf), lambda i, f: (l, 0, cur(f))),
                  pl.BlockSpec((None, D, bf), lambda i, f: (l, 0, cur(f))),
                  pl.BlockSpec((None, bf, D), lambda i, f: (l, jnp.maximum(f - 1, 0), 0)),
                  pl.BlockSpec((None, FFN_CONV, bf), lambda i, f: (l, 0, cur(f))),
                  pl.BlockSpec((None, 1, bf), lambda i, f: (l, 0, cur(f)))],
        out_specs=[pl.BlockSpec((bm, D), lambda i, f: (i, 0)),
                   pl.BlockSpec((1, nf, 2, bf), lambda i, f: (i // bps, 0, 0, 0))],
        out_shape=[jax.ShapeDtypeStruct((M, D), F32), jax.ShapeDtypeStruct((nseq, nf, 2, bf), F32)],
        scratch_shapes=[pltpu.VMEM((nf, 8, bf), F32), pltpu.VMEM((bm, bf), BF16)],
        compiler_params=_cparams("arbitrary", "arbitrary"), name="ffn_seq",
    )(h, wg, wu, wd, cw, cb.reshape(cb.shape[0], 1, F))
    return y, fst.transpose(0, 2, 1, 3).reshape(nseq, 2, F)


def _ffn_dec_body(h_ref, wg_ref, wu_ref, wd_ref, cw_ref, cb_ref, s0_ref, s1_ref, o_ref, g_ref):
    f = pl.program_id(0)
    h = h_ref[...]
    g = _dot(h, wg_ref[...])
    u = _dot(h, wu_ref[...])
    g_ref[...] = g
    cw = cw_ref[...]
    a = s0_ref[...] * cw[0:1, :] + s1_ref[...] * cw[1:2, :] + g * cw[2:3, :] + cb_ref[...]
    d = _dot(_ffn_act(a, u), wd_ref[...])

    @pl.when(f == 0)
    def _():
        o_ref[...] = d

    @pl.when(f != 0)
    def _():
        o_ref[...] += d


def ffn_dec(h, wg, wu, wd, cw, cb, st, l, bf=256):
    B, D = h.shape
    F = wg.shape[2]
    nf = F // bf
    st2 = st.reshape(st.shape[0], B, 2 * F)
    return pl.pallas_call(
        _ffn_dec_body, grid=(nf,),
        in_specs=[pl.BlockSpec((B, D), lambda f: (0, 0)),
                  pl.BlockSpec((None, D, bf), lambda f: (l, 0, f)),
                  pl.BlockSpec((None, D, bf), lambda f: (l, 0, f)),
                  pl.BlockSpec((None, bf, D), lambda f: (l, f, 0)),
                  pl.BlockSpec((None, FFN_CONV, bf), lambda f: (l, 0, f)),
                  pl.BlockSpec((None, 1, bf), lambda f: (l, 0, f)),
                  pl.BlockSpec((None, B, bf), lambda f: (l, 0, f)),
                  pl.BlockSpec((None, B, bf), lambda f: (l, 0, nf + f))],
        out_specs=[pl.BlockSpec((B, D), lambda f: (0, 0)), pl.BlockSpec((B, bf), lambda f: (0, f))],
        out_shape=[jax.ShapeDtypeStruct((B, D), F32), jax.ShapeDtypeStruct((B, F), F32)],
        compiler_params=_cparams("arbitrary"), name="ffn_dec",
    )(h, wg, wu, wd, cw, cb.reshape(cb.shape[0], 1, F), st2, st2)


def _log_gamma(h):
    return math.log(1.0 - 2.0 ** (-5.0 - h))


def _rope_tables(pos, half):
    inv = ROPE_THETA ** (-jnp.arange(half, dtype=F32) / half)
    ang = pos.astype(F32)[:, None] * inv[None, :]
    cos, sin = jnp.cos(ang), jnp.sin(ang)
    return jnp.concatenate([cos, cos], axis=-1), jnp.concatenate([-sin, sin], axis=-1)


def _rope(x, cosf, sins):
    return x * cosf + pltpu.roll(x, x.shape[-1] // 2, 1) * sins


def _ret_seq_body(q_ref, k_ref, v_ref, g_ref, cos_ref, sin_ref, o_ref, so_ref, s_ref, *, C, dk, dv):
    j = pl.program_id(1)

    @pl.when(j == 0)
    def _():
        s_ref[...] = jnp.zeros(s_ref.shape, F32)

    cosf, sins = cos_ref[...], sin_ref[...]
    ri = lax.broadcasted_iota(jnp.int32, (C, C), 0)
    ci = lax.broadcasted_iota(jnp.int32, (C, C), 1)
    idx = lax.broadcasted_iota(jnp.int32, (C, 1), 0).astype(F32)
    for h in range(RET_HEADS):
        lg = _log_gamma(h)
        q = _rope(q_ref[:, h * dk:(h + 1) * dk].astype(F32), cosf, sins)
        k = _rope(k_ref[:, h * dk:(h + 1) * dk].astype(F32), cosf, sins) * (dk ** -0.5)
        v = v_ref[:, h * dv:(h + 1) * dv]
        dmask = jnp.exp(jnp.where(ri >= ci, (ri - ci).astype(F32) * lg, NEG_BIG))
        scores = _dot_nt(q.astype(BF16), k.astype(BF16)) * dmask
        S = s_ref[h]
        qd = q * jnp.exp((idx + 1.0) * lg)
        o = _dot(scores.astype(BF16), v) + _dot(qd.astype(BF16), S.astype(BF16))
        kd = k * jnp.exp((C - 1.0 - idx) * lg)
        s_ref[h] = S * math.exp(C * lg) + _dot_tn(kd.astype(BF16), v)
        o = o * lax.rsqrt(jnp.mean(o * o, axis=-1, keepdims=True) + EPS)
        o_ref[:, h * dv:(h + 1) * dv] = (o * _silu(g_ref[:, h * dv:(h + 1) * dv].astype(F32))).astype(o_ref.dtype)

    @pl.when(j == pl.num_programs(1) - 1)
    def _():
        so_ref[0] = s_ref[...]


def retention_seq(z, B, T, dk, dv, C=256):
    C = _blk(T, C)
    nc = T // C
    qw, vw = RET_HEADS * dk, RET_HEADS * dv
    cosf, sins = _rope_tables(jnp.arange(T), dk // 2)
    body = functools.partial(_ret_seq_body, C=C, dk=dk, dv=dv)
    rows = lambda b, j: b * nc + j
    return pl.pallas_call(
        body, grid=(B, nc),
        in_specs=[pl.BlockSpec((C, qw), lambda b, j: (rows(b, j), 0)),
                  pl.BlockSpec((C, qw), lambda b, j: (rows(b, j), 1)),
                  pl.BlockSpec((C, vw), lambda b, j: (rows(b, j), (2 * qw) // vw)),
                  pl.BlockSpec((C, vw), lambda b, j: (rows(b, j), (2 * qw) // vw + 1)),
                  pl.BlockSpec((C, dk), lambda b, j: (j, 0)),
                  pl.BlockSpec((C, dk), lambda b, j: (j, 0))],
        out_specs=[pl.BlockSpec((C, vw), lambda b, j: (rows(b, j), 0)),
                   pl.BlockSpec((1, RET_HEADS, dk, dv), lambda b, j: (b, 0, 0, 0))],
        out_shape=[jax.ShapeDtypeStruct((B * T, vw), BF16), jax.ShapeDtypeStruct((B, RET_HEADS, dk, dv), F32)],
        scratch_shapes=[pltpu.VMEM((RET_HEADS, dk, dv), F32)],
        compiler_params=_cparams("arbitrary", "arbitrary"), name="retention_seq",
    )(z, z, z, z, cosf, sins)


def _row0(x, rows=8):
    r = lax.broadcasted_iota(jnp.int32, (rows, x.shape[1]), 0)
    return jnp.where(r == 0, jnp.broadcast_to(x, (rows, x.shape[1])), 0.0)


def _ret_dec_body(q_ref, k_ref, v_ref, g_ref, cos_ref, sin_ref, s_ref, o_ref, so_ref, *, bb, dk, dv):
    cosf, sins = cos_ref[...], sin_ref[...]
    for h in range(RET_HEADS):
        gamma = math.exp(_log_gamma(h))
        q = _rope(q_ref[:, h * dk:(h + 1) * dk].astype(F32), cosf, sins)
        k = _rope(k_ref[:, h * dk:(h + 1) * dk].astype(F32), cosf, sins) * (dk ** -0.5)
        v = v_ref[:, h * dv:(h + 1) * dv].astype(F32)
        gate = _silu(g_ref[:, h * dv:(h + 1) * dv].astype(F32))
        for b in range(bb):
            Sn = s_ref[b, h] * gamma + _dot_tn(_row0(k[b:b + 1]).astype(BF16), _row0(v[b:b + 1]).astype(BF16))
            so_ref[b, h] = Sn
            o = _dot(_row0(q[b:b + 1]).astype(BF16), Sn.astype(BF16))[0:1]
            o = o * lax.rsqrt(jnp.mean(o * o, axis=-1, keepdims=True) + EPS)
            o_ref[b:b + 1, h * dv:(h + 1) * dv] = (o * gate[b:b + 1]).astype(o_ref.dtype)


def retention_dec(z, state, l, dk, dv, bb=8):
    B = z.shape[0]
    bb = _blk(B, bb)
    qw, vw = RET_HEADS * dk, RET_HEADS * dv
    cosf, sins = _rope_tables(jnp.full((1,), PAST_LEN), dk // 2)
    body = functools.partial(_ret_dec_body, bb=bb, dk=dk, dv=dv)
    st = pl.BlockSpec((bb, RET_HEADS, dk, dv), lambda i: (i, 0, 0, 0))
    st_in = pl.BlockSpec((None, bb, RET_HEADS, dk, dv), lambda i: (l, i, 0, 0, 0))
    return pl.pallas_call(
        body, grid=(B // bb,),
        in_specs=[pl.BlockSpec((bb, qw), lambda i: (i, 0)),
                  pl.BlockSpec((bb, qw), lambda i: (i, 1)),
                  pl.BlockSpec((bb, vw), lambda i: (i, (2 * qw) // vw)),
                  pl.BlockSpec((bb, vw), lambda i: (i, (2 * qw) // vw + 1)),
                  pl.BlockSpec((1, dk), lambda i: (0, 0)),
                  pl.BlockSpec((1, dk), lambda i: (0, 0)), st_in],
        out_specs=[pl.BlockSpec((bb, vw), lambda i: (i, 0)), st],
        out_shape=[jax.ShapeDtypeStruct((B, vw), BF16), jax.ShapeDtypeStruct(state.shape[1:], F32)],
        compiler_params=_cparams("parallel"), name="retention_dec",
    )(z, z, z, z, cosf, sins, state)


def _pool_seq_body(u_ref, w_ref, ls_ref, o_ref, ext_ref, *, R, gw):
    j = pl.program_id(1)

    @pl.when(j == 0)
    def _():
        ext_ref[0:16, :] = jnp.zeros((16, ext_ref.shape[1]), F32)

    ext_ref[16:16 + R, :] = u_ref[...].astype(F32)
    pos = j * R + lax.broadcasted_iota(jnp.int32, (R, 1), 0)
    for gi, win in enumerate(POOL_WINDOWS):
        sl = slice(gi * gw, (gi + 1) * gw)
        cur = ext_ref[16:16 + R, sl]
        acc = cur
        for s in range(1, win):
            acc = acc + ext_ref[16 - s:16 - s + R, sl]
        cnt = jnp.minimum(pos + 1, win).astype(F32)
        d = acc / cnt - cur
        o_ref[:, sl] = (_dot(d.astype(BF16), w_ref[gi]) * ls_ref[:, sl]).astype(o_ref.dtype)
    ext_ref[0:16, :] = ext_ref[R:R + 16, :]


def pool_seq(z, w_pool, ls, l, B, T, col_block, R=256):
    R = _blk(T, R)
    nr = T // R
    gw = w_pool.shape[-1]
    W = gw * len(POOL_WINDOWS)
    body = functools.partial(_pool_seq_body, R=R, gw=gw)
    return pl.pallas_call(
        body, grid=(B, nr),
        in_specs=[pl.BlockSpec((R, W), lambda b, j: (b * nr + j, col_block)),
                  pl.BlockSpec((None,) + w_pool.shape[1:], lambda b, j: (l, 0, 0, 0)),
                  pl.BlockSpec((None, 1, W), lambda b, j: (l, 0, 0))],
        out_specs=pl.BlockSpec((R, W), lambda b, j: (b * nr + j, 0)),
        out_shape=jax.ShapeDtypeStruct((B * T, W), BF16),
        scratch_shapes=[pltpu.VMEM((16 + R, W), F32)],
        compiler_params=_cparams("arbitrary", "arbitrary"), name="pool_seq",
    )(z, w_pool, ls.reshape(-1, 1, W))


def _pool_dec_body(u_ref, st_ref, w_ref, ls_ref, o_ref, *, gw, W):
    for gi, win in enumerate(POOL_WINDOWS):
        sl = slice(gi * gw, (gi + 1) * gw)
        cur = u_ref[:, sl].astype(F32)
        acc = cur
        for s in range(1, win):
            r = POOL_BUF - s
            acc = acc + st_ref[:, r * W + gi * gw:r * W + (gi + 1) * gw]
        cnt = float(min(PAST_LEN + 1, win))
        d = acc / cnt - cur
        o_ref[:, sl] = (_dot(d.astype(BF16), w_ref[gi]) * ls_ref[:, sl]).astype(o_ref.dtype)


def pool_dec(z, state, w_pool, ls, l, col_block, bb=64):
    B = z.shape[0]
    bb = _blk(B, bb)
    gw = w_pool.shape[-1]
    W = gw * len(POOL_WINDOWS)
    body = functools.partial(_pool_dec_body, gw=gw, W=W)
    return pl.pallas_call(
        body, grid=(B // bb,),
        in_specs=[pl.BlockSpec((bb, W), lambda i: (i, col_block)),
                  pl.BlockSpec((None, bb, POOL_BUF * W), lambda i: (l, i, 0)),
                  pl.BlockSpec((None,) + w_pool.shape[1:], lambda i: (l, 0, 0, 0)),
                  pl.BlockSpec((None, 1, W), lambda i: (l, 0, 0))],
        out_specs=pl.BlockSpec((bb, W), lambda i: (i, 0)),
        out_shape=jax.ShapeDtypeStruct((B, W), BF16),
        compiler_params=_cparams("parallel"), name="pool_dec",
    )(z, state.reshape(-1, B, POOL_BUF * W), w_pool, ls.reshape(-1, 1, W))


def _l2norm(x):
    return x * lax.rsqrt(jnp.sum(x * x, axis=-1, keepdims=True) + EPS)


def _softplus(x):
    return jnp.maximum(x, 0.0) + jnp.log1p(jnp.exp(-jnp.abs(x)))


def _sigmoid(x):
    return 1.0 / (1.0 + jnp.exp(-x))


def _solve_unit_lower(Ls, R, W):
    c = Ls[0].shape[0]
    n = len(Ls)
    half = GDN_SUB // 2
    blocks = []
    for a in range(c // GDN_SUB):
        r0 = a * GDN_SUB
        Rb = R[r0:r0 + GDN_SUB, :]
        if a:
            Xp = jnp.concatenate(blocks + [jnp.zeros((c - r0, n * W), F32)], axis=0).astype(BF16)
            Rb = Rb - jnp.concatenate(
                [_dot(Ls[h][r0:r0 + GDN_SUB, :].astype(BF16), Xp[:, h * W:(h + 1) * W]) for h in range(n)], axis=1)
        top, bot = Rb[:half], Rb[half:]
        for jj in range(GDN_SUB - 1):
            coef = jnp.concatenate(
                [jnp.broadcast_to(Ls[h][r0:r0 + GDN_SUB, r0 + jj:r0 + jj + 1], (GDN_SUB, W)) for h in range(n)], axis=1)
            row = top[jj:jj + 1] if jj < half else bot[jj - half:jj - half + 1]
            if jj < half - 1:
                top = top - coef[:half] * row
            bot = bot - coef[half:] * row
        blocks.append(jnp.concatenate([top, bot], axis=0))
    return jnp.concatenate(blocks, axis=0)


def _gdn_seq_body(q_ref, k_ref, v_ref, z_ref, wq_ref, wk_ref, wv_ref, ba_ref, al_ref, dt_ref, nw_ref,
                  o_ref, so_ref, s_ref, eq_ref, ek_ref, ev_ref, cq_ref, ck_ref, cv_ref, *, R, dk, dv):
    r = pl.program_id(2)
    c = GDN_CHUNK
    hb = GDN_HB

    @pl.when(r == 0)
    def _():
        s_ref[...] = jnp.zeros(s_ref.shape, F32)
        for e in (eq_ref, ek_ref, ev_ref):
            e[...] = jnp.zeros(e.shape, F32)

    row8 = lax.broadcasted_iota(jnp.int32, eq_ref.shape, 0)
    for x_ref, w_ref, e_ref, c_ref in ((q_ref, wq_ref, eq_ref, cq_ref), (k_ref, wk_ref, ek_ref, ck_ref),
                                       (v_ref, wv_ref, ev_ref, cv_ref)):
        xf = x_ref[...].astype(F32)
        prev = e_ref[...]
        w = w_ref[...]
        acc = xf * w[GDN_CONV - 1:GDN_CONV, :]
        for s in range(1, GDN_CONV):
            xs = pltpu.roll(xf, s, 0)
            top = jnp.where(row8 < s, pltpu.roll(prev, s, 0), xs[0:8])
            xs = jnp.concatenate([top, xs[8:]], axis=0)
            acc = acc + xs * w[GDN_CONV - 1 - s:GDN_CONV - s, :]
        c_ref[...] = _silu(acc)
        e_ref[...] = xf[R - 8:R]

    ri = lax.broadcasted_iota(jnp.int32, (c, c), 0)
    ci = lax.broadcasted_iota(jnp.int32, (c, c), 1)
    tril = (ri >= ci).astype(F32)
    triu = (ri <= ci).astype(F32)
    neg_a = -jnp.exp(al_ref[0])
    dtb = dt_ref[0]
    nw = nw_ref[...]

    def chunk(ic, carry):
        r0 = pl.multiple_of(ic * c, c)
        rows = pl.ds(r0, c)
        x = ba_ref[rows, :]
        beta_all = _sigmoid(x)
        g_all = neg_a * _softplus(x + dtb)
        b_col = jnp.dot(tril, g_all, precision=lax.Precision.HIGHEST, preferred_element_type=F32)
        b_row = lax.dot_general(g_all, triu, (((0,), (0,)), ((), ())), precision=lax.Precision.HIGHEST,
                                preferred_element_type=F32)
        W = dv + dk
        qs, ks, Ls, attns, ebs, bcs, rhs, gates = [], [], [], [], [], [], [], []
        for hh in range(hb):
            hs = slice(hh * dk, (hh + 1) * dk)
            q = _l2norm(cq_ref[rows, hs]) * (dk ** -0.5)
            k = _l2norm(ck_ref[rows, hs])
            v = cv_ref[rows, hh * dv:(hh + 1) * dv]
            gates.append(_silu(z_ref[rows, hh * dv:(hh + 1) * dv].astype(F32)))
            bc = b_col[:, hb + hh:hb + hh + 1]
            br = b_row[hb + hh:hb + hh + 1, :]
            beta = beta_all[:, hh:hh + 1]
            decay = jnp.exp(jnp.where(ri >= ci, bc - br, NEG_BIG))
            kb = k * beta
            kbf = k.astype(BF16)
            Ls.append(jnp.where(ri > ci, _dot_nt(kb.astype(BF16), kbf) * decay, 0.0))
            attns.append((_dot_nt(q.astype(BF16), kbf) * decay).astype(BF16))
            eb = jnp.exp(bc)
            rhs.append(jnp.concatenate([v * beta, kb * eb], axis=1))
            qs.append(q)
            ks.append(k)
            ebs.append(eb)
            bcs.append(bc)
        uws = [_solve_unit_lower(Ls[g0:g0 + GDN_SOLVE], jnp.concatenate(rhs[g0:g0 + GDN_SOLVE], axis=1), W)
               for g0 in range(0, hb, GDN_SOLVE)]
        outs, states = [], []
        for hh in range(hb):
            uw = uws[hh // GDN_SOLVE]
            o0 = (hh % GDN_SOLVE) * W
            u, w = uw[:, o0:o0 + dv], uw[:, o0 + dv:o0 + W]
            q, k, bc = qs[hh], ks[hh], bcs[hh]
            S = s_ref[hh]
            Sb = S.astype(BF16)
            v_new = u - _dot(w.astype(BF16), Sb)
            vnb = v_new.astype(BF16)
            o = _dot((q * ebs[hh]).astype(BF16), Sb) + _dot(attns[hh], vnb)
            bl = bc[c - 1:c, :]
            states.append(S * jnp.exp(bl) + _dot_tn((k * jnp.exp(bl - bc)).astype(BF16), vnb))
            o = o * lax.rsqrt(jnp.mean(o * o, axis=-1, keepdims=True) + EPS) * nw
            outs.append((o * gates[hh]).astype(o_ref.dtype))
        for hh in range(hb):
            s_ref[hh] = states[hh]
            o_ref[rows, hh * dv:(hh + 1) * dv] = outs[hh]
        return carry

    lax.fori_loop(0, R // c, chunk, 0)

    @pl.when(r == pl.num_programs(2) - 1)
    def _():
        so_ref[0] = s_ref[...]


def _group_lanes(t):
    hb, HG = GDN_HB, GDN_HEADS // GDN_HB
    t = t.reshape(t.shape[:-1] + (HG, hb))
    t = jnp.pad(t, [(0, 0)] * (t.ndim - 1) + [(hb, LANES - 2 * hb)])
    return t.reshape(t.shape[:-2] + (HG * LANES,))


def gdn_seq(z, zba, w_gconv, a_log, dt_bias, norm_w, l, B, T, dk, dv, col0, R=256):
    H, hb = GDN_HEADS, GDN_HB
    HG = H // hb
    R = _blk(T, R)
    nr = T // R
    M = B * T
    bw = hb * dk
    cb0 = col0 // bw
    per = (H * dk) // bw
    body = functools.partial(_gdn_seq_body, R=R, dk=dk, dv=dv)
    rows = lambda b, g, r: b * nr + r
    zspec = lambda off: pl.BlockSpec((R, bw), lambda b, g, r: (rows(b, g, r), cb0 + off + g))
    wspec = lambda off: pl.BlockSpec((None, GDN_CONV, bw), lambda b, g, r: (l, 0, off + g))
    pspec = pl.BlockSpec((1, 1, LANES), lambda b, g, r: (l * HG + g, 0, 0))
    return pl.pallas_call(
        body, grid=(B, HG, nr),
        in_specs=[zspec(0), zspec(per), zspec(2 * per), zspec(3 * per),
                  wspec(0), wspec(per), wspec(2 * per),
                  pl.BlockSpec((R, LANES), lambda b, g, r: (rows(b, g, r), g)), pspec, pspec,
                  pl.BlockSpec((None, 1, dv), lambda b, g, r: (l, 0, 0))],
        out_specs=[pl.BlockSpec((R, bw), lambda b, g, r: (rows(b, g, r), g)),
                   pl.BlockSpec((1, hb, dk, dv), lambda b, g, r: (b, g, 0, 0))],
        out_shape=[jax.ShapeDtypeStruct((M, H * dv), BF16), jax.ShapeDtypeStruct((B, H, dk, dv), F32)],
        scratch_shapes=[pltpu.VMEM((hb, dk, dv), F32)] + [pltpu.VMEM((8, bw), F32)] * 3
                       + [pltpu.VMEM((R, bw), F32)] * 3,
        compiler_params=_cparams("arbitrary", "arbitrary", "arbitrary"), name="gdn_seq",
    )(z, z, z, z, w_gconv, w_gconv, w_gconv, zba,
      _group_lanes(a_log).reshape(-1, 1, LANES), _group_lanes(dt_bias).reshape(-1, 1, LANES),
      norm_w.reshape(-1, 1, dv))


def _gdn_dec_body(xq_ref, xk_ref, xv_ref, z_ref, cs_ref, w_ref, b_ref, a_ref, al_ref, dt_ref, nw_ref, s_ref,
                  o_ref, so_ref, cu_ref, *, bb, dk, dv, CH):
    H = GDN_HEADS
    GW = CH // 3
    for gi, x_ref in enumerate((xq_ref, xk_ref, xv_ref)):
        cs = slice(gi * GW, (gi + 1) * GW)
        acc = x_ref[...].astype(F32) * w_ref[GDN_CONV - 1:GDN_CONV, cs]
        for i in range(GDN_CONV - 1):
            acc = acc + cs_ref[:, i * CH + gi * GW:i * CH + (gi + 1) * GW] * w_ref[i:i + 1, cs]
        cu_ref[:, cs] = _silu(acc)
    beta = _sigmoid(b_ref[...])
    eg = jnp.exp(-jnp.exp(al_ref[...]) * _softplus(a_ref[...] + dt_ref[...]))
    nw = nw_ref[...]
    qo, ko, vo = 0, H * dk, 2 * H * dk
    for h in range(H):
        q = _l2norm(cu_ref[:, qo + h * dk:qo + (h + 1) * dk]) * (dk ** -0.5)
        k = _l2norm(cu_ref[:, ko + h * dk:ko + (h + 1) * dk])
        v = cu_ref[:, vo + h * dv:vo + (h + 1) * dv]
        qk = jnp.sum(q * k, axis=-1, keepdims=True)
        gate = _silu(z_ref[:, h * dv:(h + 1) * dv].astype(F32))
        for b in range(bb):
            S = s_ref[b, h]
            kq = jnp.concatenate([k[b:b + 1], q[b:b + 1], jnp.zeros((6, dk), F32)], axis=0)
            ks_qs = _dot(kq.astype(BF16), S.astype(BF16))
            e = eg[b:b + 1, h:h + 1]
            v_new = beta[b:b + 1, h:h + 1] * (v[b:b + 1] - e * ks_qs[0:1])
            o = e * ks_qs[1:2] + qk[b:b + 1] * v_new
            so_ref[b, h] = S * e + _dot_tn(_row0(k[b:b + 1]).astype(BF16), _row0(v_new).astype(BF16))
            o = o * lax.rsqrt(jnp.mean(o * o, axis=-1, keepdims=True) + EPS) * nw
            o_ref[b:b + 1, h * dv:(h + 1) * dv] = (o * gate[b:b + 1]).astype(o_ref.dtype)


def gdn_dec(z, zb, za, conv_state, w_gconv, a_log, dt_bias, norm_w, state, l, dk, dv, col0, bb=8):
    B = z.shape[0]
    H = GDN_HEADS
    bb = _blk(B, bb)
    CH = 3 * H * dk
    GW = H * dk
    cb0 = col0 // GW
    body = functools.partial(_gdn_dec_body, bb=bb, dk=dk, dv=dv, CH=CH)
    st = pl.BlockSpec((bb, H, dk, dv), lambda i: (i, 0, 0, 0))
    st_in = pl.BlockSpec((None, bb, H, dk, dv), lambda i: (l, i, 0, 0, 0))
    hrow = pl.BlockSpec((bb, H), lambda i: (i, 0))
    hpar = pl.BlockSpec((None, 1, H), lambda i: (l, 0, 0))
    zspec = lambda off: pl.BlockSpec((bb, GW), lambda i: (i, cb0 + off))
    return pl.pallas_call(
        body, grid=(B // bb,),
        in_specs=[zspec(0), zspec(1), zspec(2), zspec(3),
                  pl.BlockSpec((None, bb, (GDN_CONV - 1) * CH), lambda i: (l, i, 0)),
                  pl.BlockSpec((None, GDN_CONV, CH), lambda i: (l, 0, 0)),
                  hrow, hrow, hpar, hpar,
                  pl.BlockSpec((None, 1, dv), lambda i: (l, 0, 0)), st_in],
        out_specs=[pl.BlockSpec((bb, H * dv), lambda i: (i, 0)), st],
        out_shape=[jax.ShapeDtypeStruct((B, H * dv), BF16), jax.ShapeDtypeStruct(state.shape[1:], F32)],
        scratch_shapes=[pltpu.VMEM((bb, CH), F32)],
        compiler_params=_cparams("parallel"), name="gdn_dec",
    )(z, z, z, z, conv_state.reshape(-1, B, (GDN_CONV - 1) * CH), w_gconv, zb, za,
      a_log.reshape(-1, 1, H), dt_bias.reshape(-1, 1, H), norm_w.reshape(-1, 1, dv), state)


def _softmax_rows(s):
    m = jnp.max(s, axis=-1, keepdims=True)
    e = jnp.exp(s - m)
    return e / jnp.sum(e, axis=-1, keepdims=True)


def _xattn_seq_body(q_ref, k_ref, v_ref, o_ref, *, hd):
    for h in range(X_HEADS):
        sl = slice(h * hd, (h + 1) * hd)
        s = _dot_nt(q_ref[:, sl], k_ref[0, :, sl].astype(BF16)) * (hd ** -0.5)
        a = _softmax_rows(s)
        o_ref[:, sl] = _dot(a.astype(BF16), v_ref[0, :, sl].astype(BF16)).astype(o_ref.dtype)


def xattn_seq(q, mk, mv, B, T, bq=512):
    W = q.shape[1]
    hd = W // X_HEADS
    bq = _blk(T, bq)
    nq = T // bq
    mem = pl.BlockSpec((1,) + mk.shape[1:], lambda b, j: (b, 0, 0))
    return pl.pallas_call(
        functools.partial(_xattn_seq_body, hd=hd), grid=(B, nq),
        in_specs=[pl.BlockSpec((bq, W), lambda b, j: (b * nq + j, 0)), mem, mem],
        out_specs=pl.BlockSpec((bq, W), lambda b, j: (b * nq + j, 0)),
        out_shape=jax.ShapeDtypeStruct((B * T, W), BF16),
        compiler_params=_cparams("parallel", "parallel"), name="xattn_seq",
    )(q, mk, mv)


def _xattn_dec_body(q_ref, k_ref, v_ref, o_ref, *, bb, hd):
    W = X_HEADS * hd
    q = q_ref[...]
    row = lax.broadcasted_iota(jnp.int32, (8, W), 0)
    head = lax.broadcasted_iota(jnp.int32, (8, W), 1) // hd
    for b in range(bb):
        qd = jnp.where(row == head, jnp.broadcast_to(q[b:b + 1].astype(F32), (8, W)), 0.0).astype(BF16)
        s = _dot_nt(qd, k_ref[b].astype(BF16)) * (hd ** -0.5)
        a = _softmax_rows(s)
        o = _dot(a.astype(BF16), v_ref[b].astype(BF16))
        o_ref[b:b + 1, :] = jnp.sum(jnp.where(row == head, o, 0.0), axis=0, keepdims=True).astype(o_ref.dtype)


def xattn_dec(q, mk, mv, l, bb=8):
    B, W = q.shape
    hd = W // X_HEADS
    bb = _blk(B, bb)
    mem = pl.BlockSpec((None, bb) + mk.shape[2:], lambda i: (l, i, 0, 0))
    return pl.pallas_call(
        functools.partial(_xattn_dec_body, bb=bb, hd=hd), grid=(B // bb,),
        in_specs=[pl.BlockSpec((bb, W), lambda i: (i, 0)), mem, mem],
        out_specs=pl.BlockSpec((bb, W), lambda i: (i, 0)),
        out_shape=jax.ShapeDtypeStruct((B, W), BF16),
        compiler_params=_cparams("parallel"), name="xattn_dec",
    )(q, mk, mv)


def _run_trunk(x, nseq, T, states, mem_k, mem_v, W, depth):
    D = x.shape[1]
    dk_r, dv_r = D // 32, D // 16
    dk_g = dv_g = D // 32
    n_main = W["w_in"].shape[2]
    pool_w = W["w_pool"].shape[-1] * len(POOL_WINDOWS)
    off_pu = 2 * RET_HEADS * dk_r + 2 * RET_HEADS * dv_r
    off_c = off_pu + pool_w
    ch = 3 * GDN_HEADS * dk_g
    gdn_par = (W["w_gconv"], W["gdn_a_log"], W["gdn_dt_bias"], W["gdn_norm"])
    ffn_par = (W["w_gate"], W["w_up"], W["w_down"], W["w_fconv"], W["b_fconv"])
    outs = [[] for _ in range(5)]
    hn = rmsnorm(x, W["norm_mix"], 0)
    for l in range(depth):
        z = matmul(hn, W["w_in"], l, out_dtype=BF16)
        zba = matmul(hn, W["w_ba"], l, out_dtype=F32)
        if states is None:
            ro, s_ret = retention_seq(z, nseq, T, dk_r, dv_r)
            po = pool_seq(z, W["w_pool"], W["ls_pool"], l, nseq, T, off_pu // pool_w)
            co, s_gdn = gdn_seq(z, zba, *gdn_par, l, nseq, T, dk_g, dv_g, off_c)
            z3 = z.reshape(nseq, T, n_main)
            s_pool = z3[:, T - POOL_BUF:, off_pu:off_pu + pool_w].astype(F32)
            s_gconv = z3[:, T - (GDN_CONV - 1):, off_c:off_c + ch].astype(F32)
        else:
            st_ret, st_pool, st_gdn, st_gconv, st_fconv = states
            zg = zba.reshape(zba.shape[0], GDN_HEADS // GDN_HB, LANES)
            zb = zg[:, :, :GDN_HB].reshape(-1, GDN_HEADS)
            za = zg[:, :, GDN_HB:2 * GDN_HB].reshape(-1, GDN_HEADS)
            ro, s_ret = retention_dec(z, st_ret, l, dk_r, dv_r)
            po = pool_dec(z, st_pool, W["w_pool"], W["ls_pool"], l, off_pu // pool_w)
            co, s_gdn = gdn_dec(z, zb, za, st_gconv, *gdn_par, st_gdn, l, dk_g, dv_g, off_c)
            s_pool = jnp.concatenate([st_pool[l, :, 1:], z[:, None, off_pu:off_pu + pool_w].astype(F32)], axis=1)
            s_gconv = jnp.concatenate([st_gconv[l, :, 1:], z[:, None, off_c:off_c + ch].astype(F32)], axis=1)
        x = matmul([ro, po, co], W["w_out"], l, res=x)
        hx = rmsnorm(x, W["norm_x"], l)
        q = matmul(hx, W["w_xq"], l, out_dtype=BF16)
        ao = xattn_seq(q, mem_k[l], mem_v[l], nseq, T) if states is None else xattn_dec(q, mem_k, mem_v, l)
        x = matmul(ao, W["w_xo"], l, res=x)
        hf = rmsnorm(x, W["norm_ffn"], l)
        if states is None:
            y, s_fconv = ffn_seq(hf, *ffn_par, l, nseq)
        else:
            y, g_new = ffn_dec(hf, *ffn_par, st_fconv, l)
            s_fconv = jnp.concatenate([st_fconv[l, :, 1:], g_new[:, None, :]], axis=1)
        if l + 1 < depth:
            x, hn = add_rmsnorm(x, y, W["norm_mix"], l + 1)
        else:
            x, hn = add_rmsnorm(x, y, W["norm_f"], 0, out_dtype=F32)
        for lst, s in zip(outs, (s_ret, s_pool, s_gdn, s_gconv, s_fconv)):
            lst.append(s)
    return hn, [jnp.stack(s) for s in outs]


def kernel(x_prompt, x_sample, state_ret, state_pool, state_gdn, state_gdn_conv, state_ffn_conv, cache_mem_k, cache_mem_v, mem_prompt, norm_mix, w_in, w_pool, ls_pool, w_gconv, gdn_a_log, gdn_dt_bias, gdn_norm, w_out, norm_x, norm_mem, w_xq, w_xk, w_xv, w_xo, norm_ffn, w_gate, w_up, w_fconv, b_fconv, w_down, norm_f):
    depth = w_in.shape[0]
    Bp, T, D = x_prompt.shape
    Bs = x_sample.shape[0]
    assert x_sample.shape[1] == 1 and D % (32 * LANES) == 0
    n_in = w_in.shape[2]
    n_main = n_in - 2 * GDN_HEADS
    hg = GDN_HEADS // GDN_HB
    w_b = w_in[:, :, n_main:n_main + GDN_HEADS].reshape(depth, D, hg, GDN_HB)
    w_a = w_in[:, :, n_main + GDN_HEADS:].reshape(depth, D, hg, GDN_HB)
    w_ba = jnp.pad(jnp.concatenate([w_b, w_a], axis=-1), ((0, 0), (0, 0), (0, 0), (0, LANES - 2 * GDN_HB)))
    w_ba = w_ba.reshape(depth, D, hg * LANES)
    W = dict(norm_mix=norm_mix, norm_x=norm_x, norm_ffn=norm_ffn, norm_f=norm_f.reshape(1, D),
             w_in=w_in[:, :, :n_main].astype(BF16), w_ba=w_ba.astype(BF16),
             w_pool=w_pool.astype(BF16), ls_pool=ls_pool, w_gconv=w_gconv,
             gdn_a_log=gdn_a_log, gdn_dt_bias=gdn_dt_bias, gdn_norm=gdn_norm,
             w_out=w_out.astype(BF16), w_xq=w_xq.astype(BF16), w_xo=w_xo.astype(BF16),
             w_gate=w_gate.astype(BF16), w_up=w_up.astype(BF16), w_down=w_down.astype(BF16),
             w_fconv=w_fconv, b_fconv=b_fconv)
    n_mem = mem_prompt.shape[1]
    memf = mem_prompt.reshape(Bp * n_mem, D)
    w_xk16, w_xv16 = w_xk.astype(BF16), w_xv.astype(BF16)
    pk, pv = [], []
    for l in range(depth):
        mn = rmsnorm(memf, norm_mem, l)
        pk.append(matmul(mn, w_xk16, l).reshape(Bp, n_mem, -1))
        pv.append(matmul(mn, w_xv16, l).reshape(Bp, n_mem, -1))
    y_p, p_st = _run_trunk(x_prompt.reshape(Bp * T, D), Bp, T, None, pk, pv, W, depth)
    mem_shape = cache_mem_k.shape[:3] + (-1,)
    y_s, s_st = _run_trunk(x_sample.reshape(Bs, D), Bs, 1,
                           (state_ret, state_pool, state_gdn, state_gdn_conv, state_ffn_conv),
                           cache_mem_k.reshape(mem_shape), cache_mem_v.reshape(mem_shape), W, depth)
    p_mem_k = jnp.stack(pk).reshape(depth, Bp, n_mem, X_HEADS, -1)
    p_mem_v = jnp.stack(pv).reshape(depth, Bp, n_mem, X_HEADS, -1)
    return (y_p.reshape(Bp, T, D), y_s.reshape(Bs, 1, D), *p_st, p_mem_k, p_mem_v, *s_st)
```

```python
import functools
import math

import jax
import jax.numpy as jnp
from jax import lax
from jax.experimental import pallas as pl
from jax.experimental.pallas import tpu as pltpu

F32 = jnp.float32
BF16 = jnp.bfloat16

EPS = 1e-6
ROPE_THETA = 10000.0
RET_HEADS = 4
GDN_HEADS = 16
X_HEADS = 4
POOL_WINDOWS = (2, 4, 8, 16)
POOL_BUF = max(POOL_WINDOWS) - 1
GDN_CONV = 4
FFN_CONV = 3
PAST_LEN = 16384

V7X_VMEM_BYTES = 64 * 1024 * 1024
VMEM_LIMIT = V7X_VMEM_BYTES - 8 * 1024 * 1024
LANES = 128
GDN_CHUNK = 64
GDN_SUB = 16
GDN_HB = 16
GDN_SOLVE = 8
NEG_BIG = -1e30


def _cparams(*sem):
    return pltpu.CompilerParams(dimension_semantics=sem, vmem_limit_bytes=VMEM_LIMIT)


def _dot(a, b):
    return jnp.dot(a, b, preferred_element_type=F32)


def _dot_nt(a, b):
    return lax.dot_general(a, b, (((1,), (1,)), ((), ())), preferred_element_type=F32)


def _dot_tn(a, b):
    return lax.dot_general(a, b, (((0,), (0,)), ((), ())), preferred_element_type=F32)


def _silu(x):
    return (0.5 * x) * (1.0 + jnp.tanh(0.5 * x))


def _blk(n, want):
    b = min(n, want)
    while n % b:
        b //= 2
    return b


def _rmsnorm_body(x_ref, g_ref, o_ref):
    x = x_ref[...]
    y = x * lax.rsqrt(jnp.mean(x * x, axis=-1, keepdims=True) + EPS)
    o_ref[...] = (y * g_ref[...]).astype(o_ref.dtype)


def _gain_spec(l, D):
    return pl.BlockSpec((None, 1, D), lambda i: (l, 0, 0))


def rmsnorm(x, g, l, out_dtype=BF16):
    M, D = x.shape
    bm = _blk(M, 256)
    return pl.pallas_call(
        _rmsnorm_body, grid=(M // bm,),
        in_specs=[pl.BlockSpec((bm, D), lambda i: (i, 0)), _gain_spec(l, D)],
        out_specs=pl.BlockSpec((bm, D), lambda i: (i, 0)),
        out_shape=jax.ShapeDtypeStruct((M, D), out_dtype),
        compiler_params=_cparams("parallel"), name="rmsnorm",
    )(x, g.reshape(-1, 1, D))


def _add_rmsnorm_body(x_ref, y_ref, g_ref, s_ref, o_ref):
    x = x_ref[...] + y_ref[...]
    s_ref[...] = x
    y = x * lax.rsqrt(jnp.mean(x * x, axis=-1, keepdims=True) + EPS)
    o_ref[...] = (y * g_ref[...]).astype(o_ref.dtype)


def add_rmsnorm(x, y, g, l, out_dtype=BF16):
    M, D = x.shape
    bm = _blk(M, 256)
    row = pl.BlockSpec((bm, D), lambda i: (i, 0))
    return pl.pallas_call(
        _add_rmsnorm_body, grid=(M // bm,),
        in_specs=[row, row, _gain_spec(l, D)],
        out_specs=[row, row],
        out_shape=[jax.ShapeDtypeStruct((M, D), F32), jax.ShapeDtypeStruct((M, D), out_dtype)],
        compiler_params=_cparams("parallel"), name="add_rmsnorm",
    )(x, y, g.reshape(-1, 1, D))


def _mm_body(*refs, widths, has_res):
    xs, w_ref = refs[:len(widths)], refs[len(widths)]
    o_ref = refs[-1]
    acc = refs[len(widths) + 1][...] if has_res else None
    k0 = 0
    for x_ref, kw in zip(xs, widths):
        d = _dot(x_ref[...], w_ref[k0:k0 + kw, :])
        acc = d if acc is None else acc + d
        k0 += kw
    o_ref[...] = acc.astype(o_ref.dtype)


def matmul(xs, w, l, res=None, out_dtype=F32, n_out=None, bm=1024, bn=1024):
    xs = list(xs) if isinstance(xs, (list, tuple)) else [xs]
    M = xs[0].shape[0]
    widths = tuple(x.shape[1] for x in xs)
    K, N = w.shape[1], n_out or w.shape[2]
    assert sum(widths) == K
    bm, bn = _blk(M, bm), _blk(N, bn)
    in_specs = [pl.BlockSpec((bm, kw), lambda i, j: (i, 0)) for kw in widths]
    in_specs.append(pl.BlockSpec((None, K, bn), lambda i, j: (l, 0, j)))
    args = xs + [w]
    if res is not None:
        in_specs.append(pl.BlockSpec((bm, bn), lambda i, j: (i, j)))
        args.append(res)
    return pl.pallas_call(
        functools.partial(_mm_body, widths=widths, has_res=res is not None),
        grid=(M // bm, N // bn), in_specs=in_specs,
        out_specs=pl.BlockSpec((bm, bn), lambda i, j: (i, j)),
        out_shape=jax.ShapeDtypeStruct((M, N), out_dtype),
        compiler_params=_cparams("parallel", "parallel"), name="matmul",
    )(*args)


def _ffn_act(a, u):
    return (0.5 * a * (1.0 + lax.erf(a * (2.0 ** -0.5))) * u).astype(BF16)


def _ffn_seq_body(h_ref, wg_ref, wu_ref, wd_ref, cw_ref, cb_ref, o_ref, fst_ref, tail_ref, act_ref, *,
                  bm, blocks_per_seq, nf):
    i = pl.program_id(0)
    f = pl.program_id(1)
    fi = jnp.minimum(f, nf - 1)

    @pl.when(f == 0)
    def _():
        act_ref[...] = jnp.zeros(act_ref.shape, act_ref.dtype)
        o_ref[...] = jnp.zeros(o_ref.shape, o_ref.dtype)

    @pl.when(i % blocks_per_seq == 0)
    def _():
        tail_ref[fi] = jnp.zeros(tail_ref.shape[1:], F32)

    o_ref[...] += _dot(act_ref[...], wd_ref[...])
    h = h_ref[...]
    g = _dot(h, wg_ref[...])
    u = _dot(h, wu_ref[...])
    prev = tail_ref[fi]
    p2, p1 = prev[0:1, :], prev[1:2, :]
    row = lax.broadcasted_iota(jnp.int32, g.shape, 0)
    s1 = jnp.where(row == 0, p1, pltpu.roll(g, 1, 0))
    s2 = jnp.where(row == 0, p2, jnp.where(row == 1, p1, pltpu.roll(g, 2, 0)))
    cw = cw_ref[...]
    a = s2 * cw[0:1, :] + s1 * cw[1:2, :] + g * cw[2:3, :] + cb_ref[...]
    last2 = g[bm - 2:bm, :]
    tail_ref[fi, 0:2, :] = last2
    fst_ref[0, fi] = last2
    act_ref[...] = _ffn_act(a, u)


def ffn_seq(h, wg, wu, wd, cw, cb, l, nseq, bm=1024, bf=256):
    M, D = h.shape
    F = wg.shape[2]
    T = M // nseq
    bm = _blk(T, bm)
    nf = F // bf
    bps = T // bm
    body = functools.partial(_ffn_seq_body, bm=bm, blocks_per_seq=bps, nf=nf)
    cur = lambda f: jnp.minimum(f, nf - 1)
    y, fst = pl.pallas_call(
        body, grid=(M // bm, nf + 1),
        in_specs=[pl.BlockSpec((bm, D), lambda i, f: (i, 0)),
                  pl.BlockSpec((None, D, bf), lambda i, f: (l, 0, cur(f))),
                  pl.BlockSpec((None, D, bf), lambda i, f: (l, 0, cur(f))),
                  pl.BlockSpec((None, bf, D), lambda i, f: (l, jnp.maximum(f - 1, 0), 0)),
                  pl.BlockSpec((None, FFN_CONV, bf), lambda i, f: (l, 0, cur(f))),
                  pl.BlockSpec((None, 1, bf), lambda i, f: (l, 0, cur(f)))],
        out_specs=[pl.BlockSpec((bm, D), lambda i, f: (i, 0), pipeline_mode=pl.Buffered(1)),
                   pl.BlockSpec((1, nf, 2, bf), lambda i, f: (i // bps, 0, 0, 0))],
        out_shape=[jax.ShapeDtypeStruct((M, D), F32), jax.ShapeDtypeStruct((nseq, nf, 2, bf), F32)],
        scratch_shapes=[pltpu.VMEM((nf, 8, bf), F32), pltpu.VMEM((bm, bf), BF16)],
        compiler_params=_cparams("arbitrary", "arbitrary"), name="ffn_seq",
    )(h, wg, wu, wd, cw, cb.reshape(cb.shape[0], 1, F))
    return y, fst.transpose(0, 2, 1, 3).reshape(nseq, 2, F)


def _ffn_dec_body(h_ref, wg_ref, wu_ref, wd_ref, cw_ref, cb_ref, s0_ref, s1_ref, o_ref, g_ref):
    f = pl.program_id(0)
    h = h_ref[...]
    g = _dot(h, wg_ref[...])
    u = _dot(h, wu_ref[...])
    g_ref[...] = g
    cw = cw_ref[...]
    a = s0_ref[...] * cw[0:1, :] + s1_ref[...] * cw[1:2, :] + g * cw[2:3, :] + cb_ref[...]
    d = _dot(_ffn_act(a, u), wd_ref[...])

    @pl.when(f == 0)
    def _():
        o_ref[...] = d

    @pl.when(f != 0)
    def _():
        o_ref[...] += d


def ffn_dec(h, wg, wu, wd, cw, cb, st, l, bf=256):
    B, D = h.shape
    F = wg.shape[2]
    nf = F // bf
    st2 = st.reshape(st.shape[0], B, 2 * F)
    return pl.pallas_call(
        _ffn_dec_body, grid=(nf,),
        in_specs=[pl.BlockSpec((B, D), lambda f: (0, 0)),
                  pl.BlockSpec((None, D, bf), lambda f: (l, 0, f)),
                  pl.BlockSpec((None, D, bf), lambda f: (l, 0, f)),
                  pl.BlockSpec((None, bf, D), lambda f: (l, f, 0)),
                  pl.BlockSpec((None, FFN_CONV, bf), lambda f: (l, 0, f)),
                  pl.BlockSpec((None, 1, bf), lambda f: (l, 0, f)),
                  pl.BlockSpec((None, B, bf), lambda f: (l, 0, f)),
                  pl.BlockSpec((None, B, bf), lambda f: (l, 0, nf + f))],
        out_specs=[pl.BlockSpec((B, D), lambda f: (0, 0)), pl.BlockSpec((B, bf), lambda f: (0, f))],
        out_shape=[jax.ShapeDtypeStruct((B, D), F32), jax.ShapeDtypeStruct((B, F), F32)],
        compiler_params=_cparams("arbitrary"), name="ffn_dec",
    )(h, wg, wu, wd, cw, cb.reshape(cb.shape[0], 1, F), st2, st2)


def _log_gamma(h):
    return math.log(1.0 - 2.0 ** (-5.0 - h))


def _rope_tables(pos, half):
    inv = ROPE_THETA ** (-jnp.arange(half, dtype=F32) / half)
    ang = pos.astype(F32)[:, None] * inv[None, :]
    cos, sin = jnp.cos(ang), jnp.sin(ang)
    return jnp.concatenate([cos, cos], axis=-1), jnp.concatenate([-sin, sin], axis=-1)


def _rope(x, cosf, sins):
    return x * cosf + pltpu.roll(x, x.shape[-1] // 2, 1) * sins


def _ret_seq_body(q_ref, k_ref, v_ref, g_ref, cos_ref, sin_ref, o_ref, so_ref, s_ref, *, C, dk, dv):
    j = pl.program_id(1)

    @pl.when(j == 0)
    def _():
        s_ref[...] = jnp.zeros(s_ref.shape, F32)

    cosf, sins = cos_ref[...], sin_ref[...]
    ri = lax.broadcasted_iota(jnp.int32, (C, C), 0)
    ci = lax.broadcasted_iota(jnp.int32, (C, C), 1)
    idx = lax.broadcasted_iota(jnp.int32, (C, 1), 0).astype(F32)
    for h in range(RET_HEADS):
        lg = _log_gamma(h)
        q = _rope(q_ref[:, h * dk:(h + 1) * dk].astype(F32), cosf, sins)
        k = _rope(k_ref[:, h * dk:(h + 1) * dk].astype(F32), cosf, sins) * (dk ** -0.5)
        v = v_ref[:, h * dv:(h + 1) * dv]
        dmask = jnp.exp(jnp.where(ri >= ci, (ri - ci).astype(F32) * lg, NEG_BIG))
        scores = _dot_nt(q.astype(BF16), k.astype(BF16)) * dmask
        S = s_ref[h]
        qd = q * jnp.exp((idx + 1.0) * lg)
        o = _dot(scores.astype(BF16), v) + _dot(qd.astype(BF16), S.astype(BF16))
        kd = k * jnp.exp((C - 1.0 - idx) * lg)
        s_ref[h] = S * math.exp(C * lg) + _dot_tn(kd.astype(BF16), v)
        o = o * lax.rsqrt(jnp.mean(o * o, axis=-1, keepdims=True) + EPS)
        o_ref[:, h * dv:(h + 1) * dv] = (o * _silu(g_ref[:, h * dv:(h + 1) * dv].astype(F32))).astype(o_ref.dtype)

    @pl.when(j == pl.num_programs(1) - 1)
    def _():
        so_ref[0] = s_ref[...]


def retention_seq(z, B, T, dk, dv, C=256):
    C = _blk(T, C)
    nc = T // C
    qw, vw = RET_HEADS * dk, RET_HEADS * dv
    cosf, sins = _rope_tables(jnp.arange(T), dk // 2)
    body = functools.partial(_ret_seq_body, C=C, dk=dk, dv=dv)
    rows = lambda b, j: b * nc + j
    return pl.pallas_call(
        body, grid=(B, nc),
        in_specs=[pl.BlockSpec((C, qw), lambda b, j: (rows(b, j), 0)),
                  pl.BlockSpec((C, qw), lambda b, j: (rows(b, j), 1)),
                  pl.BlockSpec((C, vw), lambda b, j: (rows(b, j), (2 * qw) // vw)),
                  pl.BlockSpec((C, vw), lambda b, j: (rows(b, j), (2 * qw) // vw + 1)),
                  pl.BlockSpec((C, dk), lambda b, j: (j, 0)),
                  pl.BlockSpec((C, dk), lambda b, j: (j, 0))],
        out_specs=[pl.BlockSpec((C, vw), lambda b, j: (rows(b, j), 0)),
                   pl.BlockSpec((1, RET_HEADS, dk, dv), lambda b, j: (b, 0, 0, 0))],
        out_shape=[jax.ShapeDtypeStruct((B * T, vw), BF16), jax.ShapeDtypeStruct((B, RET_HEADS, dk, dv), F32)],
        scratch_shapes=[pltpu.VMEM((RET_HEADS, dk, dv), F32)],
        compiler_params=_cparams("arbitrary", "arbitrary"), name="retention_seq",
    )(z, z, z, z, cosf, sins)


def _row0(x, rows=8):
    r = lax.broadcasted_iota(jnp.int32, (rows, x.shape[1]), 0)
    return jnp.where(r == 0, jnp.broadcast_to(x, (rows, x.shape[1])), 0.0)


def _ret_dec_body(q_ref, k_ref, v_ref, g_ref, cos_ref, sin_ref, s_ref, acc_ref, o_ref, so_ref, *, bb, dk, dv):
    cosf, sins = cos_ref[...], sin_ref[...]
    for h in range(RET_HEADS):
        gamma = math.exp(_log_gamma(h))
        q = _rope(q_ref[:, h * dk:(h + 1) * dk].astype(F32), cosf, sins)
        k = _rope(k_ref[:, h * dk:(h + 1) * dk].astype(F32), cosf, sins) * (dk ** -0.5)
        v = v_ref[:, h * dv:(h + 1) * dv].astype(F32)
        gate = _silu(g_ref[:, h * dv:(h + 1) * dv].astype(F32))
        for b in range(bb):
            Sn = s_ref[b, h] * gamma + _dot_tn(_row0(k[b:b + 1]).astype(BF16), _row0(v[b:b + 1]).astype(BF16))
            so_ref[b, h] = Sn
            o = _dot(_row0(q[b:b + 1]).astype(BF16), Sn.astype(BF16))[0:1]
            o = o * lax.rsqrt(jnp.mean(o * o, axis=-1, keepdims=True) + EPS)
            o_ref[b:b + 1, h * dv:(h + 1) * dv] = (o * gate[b:b + 1]).astype(o_ref.dtype)


def retention_dec(z, state, acc, l, dk, dv, bb=8):
    B = z.shape[0]
    bb = _blk(B, bb)
    qw, vw = RET_HEADS * dk, RET_HEADS * dv
    cosf, sins = _rope_tables(jnp.full((1,), PAST_LEN), dk // 2)
    body = functools.partial(_ret_dec_body, bb=bb, dk=dk, dv=dv)
    st = pl.BlockSpec((None, bb, RET_HEADS, dk, dv), lambda i: (l, i, 0, 0, 0))
    return pl.pallas_call(
        body, grid=(B // bb,),
        in_specs=[pl.BlockSpec((bb, qw), lambda i: (i, 0)),
                  pl.BlockSpec((bb, qw), lambda i: (i, 1)),
                  pl.BlockSpec((bb, vw), lambda i: (i, (2 * qw) // vw)),
                  pl.BlockSpec((bb, vw), lambda i: (i, (2 * qw) // vw + 1)),
                  pl.BlockSpec((1, dk), lambda i: (0, 0)),
                  pl.BlockSpec((1, dk), lambda i: (0, 0)), st, pl.BlockSpec(memory_space=pl.ANY)],
        out_specs=[pl.BlockSpec((bb, vw), lambda i: (i, 0)), st],
        out_shape=[jax.ShapeDtypeStruct((B, vw), BF16), jax.ShapeDtypeStruct(state.shape, F32)],
        input_output_aliases={7: 1},
        compiler_params=_cparams("parallel"), name="retention_dec",
    )(z, z, z, z, cosf, sins, state, acc)


def _pool_seq_body(u_ref, w_ref, ls_ref, o_ref, ext_ref, *, R, gw):
    j = pl.program_id(1)

    @pl.when(j == 0)
    def _():
        ext_ref[0:16, :] = jnp.zeros((16, ext_ref.shape[1]), F32)

    ext_ref[16:16 + R, :] = u_ref[...].astype(F32)
    pos = j * R + lax.broadcasted_iota(jnp.int32, (R, 1), 0)
    for gi, win in enumerate(POOL_WINDOWS):
        sl = slice(gi * gw, (gi + 1) * gw)
        cur = ext_ref[16:16 + R, sl]
        acc = cur
        for s in range(1, win):
            acc = acc + ext_ref[16 - s:16 - s + R, sl]
        cnt = jnp.minimum(pos + 1, win).astype(F32)
        d = acc / cnt - cur
        o_ref[:, sl] = (_dot(d.astype(BF16), w_ref[gi]) * ls_ref[:, sl]).astype(o_ref.dtype)
    ext_ref[0:16, :] = ext_ref[R:R + 16, :]


def pool_seq(z, w_pool, ls, l, B, T, col_block, R=256):
    R = _blk(T, R)
    nr = T // R
    gw = w_pool.shape[-1]
    W = gw * len(POOL_WINDOWS)
    body = functools.partial(_pool_seq_body, R=R, gw=gw)
    return pl.pallas_call(
        body, grid=(B, nr),
        in_specs=[pl.BlockSpec((R, W), lambda b, j: (b * nr + j, col_block)),
                  pl.BlockSpec((None,) + w_pool.shape[1:], lambda b, j: (l, 0, 0, 0)),
                  pl.BlockSpec((None, 1, W), lambda b, j: (l, 0, 0))],
        out_specs=pl.BlockSpec((R, W), lambda b, j: (b * nr + j, 0)),
        out_shape=jax.ShapeDtypeStruct((B * T, W), BF16),
        scratch_shapes=[pltpu.VMEM((16 + R, W), F32)],
        compiler_params=_cparams("arbitrary", "arbitrary"), name="pool_seq",
    )(z, w_pool, ls.reshape(-1, 1, W))


def _pool_dec_body(u_ref, st_ref, w_ref, ls_ref, o_ref, *, gw, W):
    for gi, win in enumerate(POOL_WINDOWS):
        sl = slice(gi * gw, (gi + 1) * gw)
        cur = u_ref[:, sl].astype(F32)
        acc = cur
        for s in range(1, win):
            r = POOL_BUF - s
            acc = acc + st_ref[:, r * W + gi * gw:r * W + (gi + 1) * gw]
        cnt = float(min(PAST_LEN + 1, win))
        d = acc / cnt - cur
        o_ref[:, sl] = (_dot(d.astype(BF16), w_ref[gi]) * ls_ref[:, sl]).astype(o_ref.dtype)


def pool_dec(z, state, w_pool, ls, l, col_block, bb=64):
    B = z.shape[0]
    bb = _blk(B, bb)
    gw = w_pool.shape[-1]
    W = gw * len(POOL_WINDOWS)
    body = functools.partial(_pool_dec_body, gw=gw, W=W)
    return pl.pallas_call(
        body, grid=(B // bb,),
        in_specs=[pl.BlockSpec((bb, W), lambda i: (i, col_block)),
                  pl.BlockSpec((None, bb, POOL_BUF * W), lambda i: (l, i, 0)),
                  pl.BlockSpec((None,) + w_pool.shape[1:], lambda i: (l, 0, 0, 0)),
                  pl.BlockSpec((None, 1, W), lambda i: (l, 0, 0))],
        out_specs=pl.BlockSpec((bb, W), lambda i: (i, 0)),
        out_shape=jax.ShapeDtypeStruct((B, W), BF16),
        compiler_params=_cparams("parallel"), name="pool_dec",
    )(z, state.reshape(-1, B, POOL_BUF * W), w_pool, ls.reshape(-1, 1, W))


def _l2norm(x):
    return x * lax.rsqrt(jnp.sum(x * x, axis=-1, keepdims=True) + EPS)


def _softplus(x):
    return jnp.maximum(x, 0.0) + jnp.log1p(jnp.exp(-jnp.abs(x)))


def _sigmoid(x):
    return 1.0 / (1.0 + jnp.exp(-x))


def _solve_unit_lower(Ls, R, W):
    c = Ls[0].shape[0]
    n = len(Ls)
    half = GDN_SUB // 2
    blocks = []
    for a in range(c // GDN_SUB):
        r0 = a * GDN_SUB
        Rb = R[r0:r0 + GDN_SUB, :]
        if a:
            Xp = jnp.concatenate(blocks + [jnp.zeros((c - r0, n * W), F32)], axis=0).astype(BF16)
            Rb = Rb - jnp.concatenate(
                [_dot(Ls[h][r0:r0 + GDN_SUB, :].astype(BF16), Xp[:, h * W:(h + 1) * W]) for h in range(n)], axis=1)
        top, bot = Rb[:half], Rb[half:]
        for jj in range(GDN_SUB - 1):
            coef = jnp.concatenate(
                [jnp.broadcast_to(Ls[h][r0:r0 + GDN_SUB, r0 + jj:r0 + jj + 1], (GDN_SUB, W)) for h in range(n)], axis=1)
            row = top[jj:jj + 1] if jj < half else bot[jj - half:jj - half + 1]
            if jj < half - 1:
                top = top - coef[:half] * row
            bot = bot - coef[half:] * row
        blocks.append(jnp.concatenate([top, bot], axis=0))
    return jnp.concatenate(blocks, axis=0)


def _gdn_seq_body(q_ref, k_ref, v_ref, z_ref, wq_ref, wk_ref, wv_ref, ba_ref, al_ref, dt_ref, nw_ref,
                  o_ref, so_ref, s_ref, eq_ref, ek_ref, ev_ref, cq_ref, ck_ref, cv_ref, *, R, dk, dv):
    r = pl.program_id(2)
    c = GDN_CHUNK
    hb = GDN_HB

    @pl.when(r == 0)
    def _():
        s_ref[...] = jnp.zeros(s_ref.shape, F32)
        for e in (eq_ref, ek_ref, ev_ref):
            e[...] = jnp.zeros(e.shape, F32)

    row8 = lax.broadcasted_iota(jnp.int32, eq_ref.shape, 0)
    for x_ref, w_ref, e_ref, c_ref in ((q_ref, wq_ref, eq_ref, cq_ref), (k_ref, wk_ref, ek_ref, ck_ref),
                                       (v_ref, wv_ref, ev_ref, cv_ref)):
        xf = x_ref[...].astype(F32)
        prev = e_ref[...]
        w = w_ref[...]
        acc = xf * w[GDN_CONV - 1:GDN_CONV, :]
        for s in range(1, GDN_CONV):
            xs = pltpu.roll(xf, s, 0)
            top = jnp.where(row8 < s, pltpu.roll(prev, s, 0), xs[0:8])
            xs = jnp.concatenate([top, xs[8:]], axis=0)
            acc = acc + xs * w[GDN_CONV - 1 - s:GDN_CONV - s, :]
        c_ref[...] = _silu(acc)
        e_ref[...] = xf[R - 8:R]

    ri = lax.broadcasted_iota(jnp.int32, (c, c), 0)
    ci = lax.broadcasted_iota(jnp.int32, (c, c), 1)
    tril = (ri >= ci).astype(F32)
    triu = (ri <= ci).astype(F32)
    neg_a = -jnp.exp(al_ref[0])
    dtb = dt_ref[0]
    nw = nw_ref[...]

    def chunk(ic, carry):
        r0 = pl.multiple_of(ic * c, c)
        rows = pl.ds(r0, c)
        x = ba_ref[rows, :]
        beta_all = _sigmoid(x)
        g_all = neg_a * _softplus(x + dtb)
        b_col = jnp.dot(tril, g_all, precision=lax.Precision.HIGHEST, preferred_element_type=F32)
        b_row = lax.dot_general(g_all, triu, (((0,), (0,)), ((), ())), precision=lax.Precision.HIGHEST,
                                preferred_element_type=F32)
        W = dv + dk
        qs, ks, Ls, attns, ebs, bcs, rhs, gates = [], [], [], [], [], [], [], []
        for hh in range(hb):
            hs = slice(hh * dk, (hh + 1) * dk)
            q = _l2norm(cq_ref[rows, hs]) * (dk ** -0.5)
            k = _l2norm(ck_ref[rows, hs])
            v = cv_ref[rows, hh * dv:(hh + 1) * dv]
            gates.append(_silu(z_ref[rows, hh * dv:(hh + 1) * dv].astype(F32)))
            bc = b_col[:, hb + hh:hb + hh + 1]
            br = b_row[hb + hh:hb + hh + 1, :]
            beta = beta_all[:, hh:hh + 1]
            decay = jnp.exp(jnp.where(ri >= ci, bc - br, NEG_BIG))
            kb = k * beta
            kbf = k.astype(BF16)
            Ls.append(jnp.where(ri > ci, _dot_nt(kb.astype(BF16), kbf) * decay, 0.0))
            attns.append((_dot_nt(q.astype(BF16), kbf) * decay).astype(BF16))
            eb = jnp.exp(bc)
            rhs.append(jnp.concatenate([v * beta, kb * eb], axis=1))
            qs.append(q)
            ks.append(k)
            ebs.append(eb)
            bcs.append(bc)
        uws = [_solve_unit_lower(Ls[g0:g0 + GDN_SOLVE], jnp.concatenate(rhs[g0:g0 + GDN_SOLVE], axis=1), W)
               for g0 in range(0, hb, GDN_SOLVE)]
        outs, states = [], []
        for hh in range(hb):
            uw = uws[hh // GDN_SOLVE]
            o0 = (hh % GDN_SOLVE) * W
            u, w = uw[:, o0:o0 + dv], uw[:, o0 + dv:o0 + W]
            q, k, bc = qs[hh], ks[hh], bcs[hh]
            S = s_ref[hh]
            Sb = S.astype(BF16)
            v_new = u - _dot(w.astype(BF16), Sb)
            vnb = v_new.astype(BF16)
            o = _dot((q * ebs[hh]).astype(BF16), Sb) + _dot(attns[hh], vnb)
            bl = bc[c - 1:c, :]
            states.append(S * jnp.exp(bl) + _dot_tn((k * jnp.exp(bl - bc)).astype(BF16), vnb))
            o = o * lax.rsqrt(jnp.mean(o * o, axis=-1, keepdims=True) + EPS) * nw
            outs.append((o * gates[hh]).astype(o_ref.dtype))
        for hh in range(hb):
            s_ref[hh] = states[hh]
            o_ref[rows, hh * dv:(hh + 1) * dv] = outs[hh]
        return carry

    lax.fori_loop(0, R // c, chunk, 0)

    @pl.when(r == pl.num_programs(2) - 1)
    def _():
        so_ref[0] = s_ref[...]


def _group_lanes(t):
    hb, HG = GDN_HB, GDN_HEADS // GDN_HB
    t = t.reshape(t.shape[:-1] + (HG, hb))
    t = jnp.pad(t, [(0, 0)] * (t.ndim - 1) + [(hb, LANES - 2 * hb)])
    return t.reshape(t.shape[:-2] + (HG * LANES,))


def gdn_seq(z, zba, w_gconv, a_log, dt_bias, norm_w, l, B, T, dk, dv, col0, R=256):
    H, hb = GDN_HEADS, GDN_HB
    HG = H // hb
    R = _blk(T, R)
    nr = T // R
    M = B * T
    bw = hb * dk
    cb0 = col0 // bw
    per = (H * dk) // bw
    body = functools.partial(_gdn_seq_body, R=R, dk=dk, dv=dv)
    rows = lambda b, g, r: b * nr + r
    zspec = lambda off: pl.BlockSpec((R, bw), lambda b, g, r: (rows(b, g, r), cb0 + off + g))
    wspec = lambda off: pl.BlockSpec((None, GDN_CONV, bw), lambda b, g, r: (l, 0, off + g))
    pspec = pl.BlockSpec((1, 1, LANES), lambda b, g, r: (l * HG + g, 0, 0))
    return pl.pallas_call(
        body, grid=(B, HG, nr),
        in_specs=[zspec(0), zspec(per), zspec(2 * per), zspec(3 * per),
                  wspec(0), wspec(per), wspec(2 * per),
                  pl.BlockSpec((R, LANES), lambda b, g, r: (rows(b, g, r), g)), pspec, pspec,
                  pl.BlockSpec((None, 1, dv), lambda b, g, r: (l, 0, 0))],
        out_specs=[pl.BlockSpec((R, bw), lambda b, g, r: (rows(b, g, r), g)),
                   pl.BlockSpec((1, hb, dk, dv), lambda b, g, r: (b, g, 0, 0))],
        out_shape=[jax.ShapeDtypeStruct((M, H * dv), BF16), jax.ShapeDtypeStruct((B, H, dk, dv), F32)],
        scratch_shapes=[pltpu.VMEM((hb, dk, dv), F32)] + [pltpu.VMEM((8, bw), F32)] * 3
                       + [pltpu.VMEM((R, bw), F32)] * 3,
        compiler_params=_cparams("arbitrary", "arbitrary", "arbitrary"), name="gdn_seq",
    )(z, z, z, z, w_gconv, w_gconv, w_gconv, zba,
      _group_lanes(a_log).reshape(-1, 1, LANES), _group_lanes(dt_bias).reshape(-1, 1, LANES),
      norm_w.reshape(-1, 1, dv))


def _gdn_dec_body(xq_ref, xk_ref, xv_ref, z_ref, cs_ref, w_ref, b_ref, a_ref, al_ref, dt_ref, nw_ref, s_ref,
                  acc_ref, o_ref, so_ref, cu_ref, *, bb, dk, dv, CH):
    H = GDN_HEADS
    GW = CH // 3
    for gi, x_ref in enumerate((xq_ref, xk_ref, xv_ref)):
        cs = slice(gi * GW, (gi + 1) * GW)
        acc = x_ref[...].astype(F32) * w_ref[GDN_CONV - 1:GDN_CONV, cs]
        for i in range(GDN_CONV - 1):
            acc = acc + cs_ref[:, i * CH + gi * GW:i * CH + (gi + 1) * GW] * w_ref[i:i + 1, cs]
        cu_ref[:, cs] = _silu(acc)
    beta = _sigmoid(b_ref[...])
    eg = jnp.exp(-jnp.exp(al_ref[...]) * _softplus(a_ref[...] + dt_ref[...]))
    nw = nw_ref[...]
    qo, ko, vo = 0, H * dk, 2 * H * dk
    for h in range(H):
        q = _l2norm(cu_ref[:, qo + h * dk:qo + (h + 1) * dk]) * (dk ** -0.5)
        k = _l2norm(cu_ref[:, ko + h * dk:ko + (h + 1) * dk])
        v = cu_ref[:, vo + h * dv:vo + (h + 1) * dv]
        qk = jnp.sum(q * k, axis=-1, keepdims=True)
        gate = _silu(z_ref[:, h * dv:(h + 1) * dv].astype(F32))
        for b in range(bb):
            S = s_ref[b, h]
            kq = jnp.concatenate([k[b:b + 1], q[b:b + 1], jnp.zeros((6, dk), F32)], axis=0)
            ks_qs = _dot(kq.astype(BF16), S.astype(BF16))
            e = eg[b:b + 1, h:h + 1]
            v_new = beta[b:b + 1, h:h + 1] * (v[b:b + 1] - e * ks_qs[0:1])
            o = e * ks_qs[1:2] + qk[b:b + 1] * v_new
            so_ref[b, h] = S * e + _dot_tn(_row0(k[b:b + 1]).astype(BF16), _row0(v_new).astype(BF16))
            o = o * lax.rsqrt(jnp.mean(o * o, axis=-1, keepdims=True) + EPS) * nw
            o_ref[b:b + 1, h * dv:(h + 1) * dv] = (o * gate[b:b + 1]).astype(o_ref.dtype)


def gdn_dec(z, zb, za, conv_state, w_gconv, a_log, dt_bias, norm_w, state, acc, l, dk, dv, col0, bb=8):
    B = z.shape[0]
    H = GDN_HEADS
    bb = _blk(B, bb)
    CH = 3 * H * dk
    GW = H * dk
    cb0 = col0 // GW
    body = functools.partial(_gdn_dec_body, bb=bb, dk=dk, dv=dv, CH=CH)
    st = pl.BlockSpec((None, bb, H, dk, dv), lambda i: (l, i, 0, 0, 0))
    hrow = pl.BlockSpec((bb, H), lambda i: (i, 0))
    hpar = pl.BlockSpec((None, 1, H), lambda i: (l, 0, 0))
    zspec = lambda off: pl.BlockSpec((bb, GW), lambda i: (i, cb0 + off))
    return pl.pallas_call(
        body, grid=(B // bb,),
        in_specs=[zspec(0), zspec(1), zspec(2), zspec(3),
                  pl.BlockSpec((None, bb, (GDN_CONV - 1) * CH), lambda i: (l, i, 0)),
                  pl.BlockSpec((None, GDN_CONV, CH), lambda i: (l, 0, 0)),
                  hrow, hrow, hpar, hpar,
                  pl.BlockSpec((None, 1, dv), lambda i: (l, 0, 0)), st, pl.BlockSpec(memory_space=pl.ANY)],
        out_specs=[pl.BlockSpec((bb, H * dv), lambda i: (i, 0)), st],
        out_shape=[jax.ShapeDtypeStruct((B, H * dv), BF16), jax.ShapeDtypeStruct(state.shape, F32)],
        input_output_aliases={12: 1},
        scratch_shapes=[pltpu.VMEM((bb, CH), F32)],
        compiler_params=_cparams("parallel"), name="gdn_dec",
    )(z, z, z, z, conv_state.reshape(-1, B, (GDN_CONV - 1) * CH), w_gconv, zb, za,
      a_log.reshape(-1, 1, H), dt_bias.reshape(-1, 1, H), norm_w.reshape(-1, 1, dv), state, acc)


def _softmax_rows(s):
    m = jnp.max(s, axis=-1, keepdims=True)
    e = jnp.exp(s - m)
    return e / jnp.sum(e, axis=-1, keepdims=True)


def _xattn_seq_body(q_ref, k_ref, v_ref, o_ref, *, hd):
    for h in range(X_HEADS):
        sl = slice(h * hd, (h + 1) * hd)
        s = _dot_nt(q_ref[:, sl], k_ref[0, :, sl].astype(BF16)) * (hd ** -0.5)
        a = _softmax_rows(s)
        o_ref[:, sl] = _dot(a.astype(BF16), v_ref[0, :, sl].astype(BF16)).astype(o_ref.dtype)


def xattn_seq(q, mk, mv, B, T, bq=512):
    W = q.shape[1]
    hd = W // X_HEADS
    bq = _blk(T, bq)
    nq = T // bq
    mem = pl.BlockSpec((1,) + mk.shape[1:], lambda b, j: (b, 0, 0))
    return pl.pallas_call(
        functools.partial(_xattn_seq_body, hd=hd), grid=(B, nq),
        in_specs=[pl.BlockSpec((bq, W), lambda b, j: (b * nq + j, 0)), mem, mem],
        out_specs=pl.BlockSpec((bq, W), lambda b, j: (b * nq + j, 0)),
        out_shape=jax.ShapeDtypeStruct((B * T, W), BF16),
        compiler_params=_cparams("parallel", "parallel"), name="xattn_seq",
    )(q, mk, mv)


def _xattn_dec_body(q_ref, k_hbm, v_hbm, o_ref, kbuf, vbuf, sem, *, l, bb, hd, nsteps):
    i = pl.program_id(0)
    W = X_HEADS * hd

    def copies(step, slot):
        out = []
        for t, (src, dst) in enumerate(((k_hbm, kbuf), (v_hbm, vbuf))):
            for h in range(X_HEADS):
                out.append(pltpu.make_async_copy(src.at[l, pl.ds(step * bb, bb), :, h, :],
                                                 dst.at[slot, :, :, pl.ds(h * hd, hd)], sem.at[t, slot, h]))
        return out

    slot = i % 2

    @pl.when(i == 0)
    def _():
        for cp in copies(0, 0):
            cp.start()

    @pl.when(i + 1 < nsteps)
    def _():
        for cp in copies(i + 1, 1 - slot):
            cp.start()

    for cp in copies(i, slot):
        cp.wait()

    q = q_ref[...]
    row = lax.broadcasted_iota(jnp.int32, (8, W), 0)
    head = lax.broadcasted_iota(jnp.int32, (8, W), 1) // hd
    for b in range(bb):
        qd = jnp.where(row == head, jnp.broadcast_to(q[b:b + 1].astype(F32), (8, W)), 0.0).astype(BF16)
        s = _dot_nt(qd, kbuf[slot, b].astype(BF16)) * (hd ** -0.5)
        a = _softmax_rows(s)
        o = _dot(a.astype(BF16), vbuf[slot, b].astype(BF16))
        o_ref[b:b + 1, :] = jnp.sum(jnp.where(row == head, o, 0.0), axis=0, keepdims=True).astype(o_ref.dtype)


def xattn_dec(q, mk, mv, l, bb=8):
    B, W = q.shape
    hd = W // X_HEADS
    n_mem = mk.shape[2]
    bb = _blk(B, bb)
    nsteps = B // bb
    hbm = pl.BlockSpec(memory_space=pl.ANY)
    return pl.pallas_call(
        functools.partial(_xattn_dec_body, l=l, bb=bb, hd=hd, nsteps=nsteps), grid=(nsteps,),
        in_specs=[pl.BlockSpec((bb, W), lambda i: (i, 0)), hbm, hbm],
        out_specs=pl.BlockSpec((bb, W), lambda i: (i, 0)),
        out_shape=jax.ShapeDtypeStruct((B, W), BF16),
        scratch_shapes=[pltpu.VMEM((2, bb, n_mem, W), F32), pltpu.VMEM((2, bb, n_mem, W), F32),
                        pltpu.SemaphoreType.DMA((2, 2, X_HEADS))],
        compiler_params=_cparams("arbitrary"), name="xattn_dec",
    )(q, mk, mv)


def _run_trunk(x, nseq, T, states, mem_k, mem_v, W, depth):
    D = x.shape[1]
    dk_r, dv_r = D // 32, D // 16
    dk_g = dv_g = D // 32
    n_main = W["w_in"].shape[2] - 2 * GDN_HEADS
    pool_w = W["w_pool"].shape[-1] * len(POOL_WINDOWS)
    off_pu = 2 * RET_HEADS * dk_r + 2 * RET_HEADS * dv_r
    off_c = off_pu + pool_w
    ch = 3 * GDN_HEADS * dk_g
    gdn_par = (W["w_gconv"], W["gdn_a_log"], W["gdn_dt_bias"], W["gdn_norm"])
    ffn_par = (W["w_gate"], W["w_up"], W["w_down"], W["w_fconv"], W["b_fconv"])
    outs = [[] for _ in range(5)]
    if states is not None:
        acc_ret, acc_gdn = lax.empty(states[0].shape, F32), lax.empty(states[2].shape, F32)
    hn = rmsnorm(x, W["norm_mix"], 0)
    for l in range(depth):
        z = matmul(hn, W["w_in"], l, out_dtype=BF16, n_out=n_main)
        zba = matmul(hn, W["w_ba"], l, out_dtype=F32)
        if states is None:
            ro, s_ret = retention_seq(z, nseq, T, dk_r, dv_r)
            po = pool_seq(z, W["w_pool"], W["ls_pool"], l, nseq, T, off_pu // pool_w)
            co, s_gdn = gdn_seq(z, zba, *gdn_par, l, nseq, T, dk_g, dv_g, off_c)
            z3 = z.reshape(nseq, T, n_main)
            s_pool = z3[:, T - POOL_BUF:, off_pu:off_pu + pool_w].astype(F32)
            s_gconv = z3[:, T - (GDN_CONV - 1):, off_c:off_c + ch].astype(F32)
        else:
            st_ret, st_pool, st_gdn, st_gconv, st_fconv = states
            zg = zba.reshape(zba.shape[0], GDN_HEADS // GDN_HB, LANES)
            zb = zg[:, :, :GDN_HB].reshape(-1, GDN_HEADS)
            za = zg[:, :, GDN_HB:2 * GDN_HB].reshape(-1, GDN_HEADS)
            ro, acc_ret = retention_dec(z, st_ret, acc_ret, l, dk_r, dv_r)
            po = pool_dec(z, st_pool, W["w_pool"], W["ls_pool"], l, off_pu // pool_w)
            co, acc_gdn = gdn_dec(z, zb, za, st_gconv, *gdn_par, st_gdn, acc_gdn, l, dk_g, dv_g, off_c)
            s_ret = s_gdn = None
            s_pool = jnp.concatenate([st_pool[l, :, 1:], z[:, None, off_pu:off_pu + pool_w].astype(F32)], axis=1)
            s_gconv = jnp.concatenate([st_gconv[l, :, 1:], z[:, None, off_c:off_c + ch].astype(F32)], axis=1)
        x = matmul([ro, po, co], W["w_out"], l, res=x)
        hx = rmsnorm(x, W["norm_x"], l)
        q = matmul(hx, W["w_xq"], l, out_dtype=BF16)
        ao = xattn_seq(q, mem_k[l], mem_v[l], nseq, T) if states is None else xattn_dec(q, mem_k, mem_v, l)
        x = matmul(ao, W["w_xo"], l, res=x)
        hf = rmsnorm(x, W["norm_ffn"], l)
        if states is None:
            y, s_fconv = ffn_seq(hf, *ffn_par, l, nseq)
        else:
            y, g_new = ffn_dec(hf, *ffn_par, st_fconv, l)
            s_fconv = jnp.concatenate([st_fconv[l, :, 1:], g_new[:, None, :]], axis=1)
        if l + 1 < depth:
            x, hn = add_rmsnorm(x, y, W["norm_mix"], l + 1)
        else:
            x, hn = add_rmsnorm(x, y, W["norm_f"], 0, out_dtype=F32)
        for lst, s in zip(outs, (s_ret, s_pool, s_gdn, s_gconv, s_fconv)):
            lst.append(s)
    if states is None:
        return hn, [jnp.stack(s) for s in outs]
    return hn, [acc_ret, jnp.stack(outs[1]), acc_gdn, jnp.stack(outs[3]), jnp.stack(outs[4])]


def kernel(x_prompt, x_sample, state_ret, state_pool, state_gdn, state_gdn_conv, state_ffn_conv, cache_mem_k, cache_mem_v, mem_prompt, norm_mix, w_in, w_pool, ls_pool, w_gconv, gdn_a_log, gdn_dt_bias, gdn_norm, w_out, norm_x, norm_mem, w_xq, w_xk, w_xv, w_xo, norm_ffn, w_gate, w_up, w_fconv, b_fconv, w_down, norm_f):
    depth = w_in.shape[0]
    Bp, T, D = x_prompt.shape
    Bs = x_sample.shape[0]
    assert x_sample.shape[1] == 1 and D % (32 * LANES) == 0
    n_in = w_in.shape[2]
    n_main = n_in - 2 * GDN_HEADS
    hg = GDN_HEADS // GDN_HB
    w_b = w_in[:, :, n_main:n_main + GDN_HEADS].reshape(depth, D, hg, GDN_HB)
    w_a = w_in[:, :, n_main + GDN_HEADS:].reshape(depth, D, hg, GDN_HB)
    w_ba = jnp.pad(jnp.concatenate([w_b, w_a], axis=-1), ((0, 0), (0, 0), (0, 0), (0, LANES - 2 * GDN_HB)))
    w_ba = w_ba.reshape(depth, D, hg * LANES)
    W = dict(norm_mix=norm_mix, norm_x=norm_x, norm_ffn=norm_ffn, norm_f=norm_f.reshape(1, D),
             w_in=w_in.astype(BF16), w_ba=w_ba.astype(BF16),
             w_pool=w_pool.astype(BF16), ls_pool=ls_pool, w_gconv=w_gconv,
             gdn_a_log=gdn_a_log, gdn_dt_bias=gdn_dt_bias, gdn_norm=gdn_norm,
             w_out=w_out.astype(BF16), w_xq=w_xq.astype(BF16), w_xo=w_xo.astype(BF16),
             w_gate=w_gate.astype(BF16), w_up=w_up.astype(BF16), w_down=w_down.astype(BF16),
             w_fconv=w_fconv, b_fconv=b_fconv)
    n_mem = mem_prompt.shape[1]
    memf = mem_prompt.reshape(Bp * n_mem, D)
    w_xk16, w_xv16 = w_xk.astype(BF16), w_xv.astype(BF16)
    pk, pv = [], []
    for l in range(depth):
        mn = rmsnorm(memf, norm_mem, l)
        pk.append(matmul(mn, w_xk16, l).reshape(Bp, n_mem, -1))
        pv.append(matmul(mn, w_xv16, l).reshape(Bp, n_mem, -1))
    y_p, p_st = _run_trunk(x_prompt.reshape(Bp * T, D), Bp, T, None, pk, pv, W, depth)
    y_s, s_st = _run_trunk(x_sample.reshape(Bs, D), Bs, 1,
                           (state_ret, state_pool, state_gdn, state_gdn_conv, state_ffn_conv),
                           cache_mem_k, cache_mem_v, W, depth)
    p_mem_k = jnp.stack(pk).reshape(depth, Bp, n_mem, X_HEADS, -1)
    p_mem_v = jnp.stack(pv).reshape(depth, Bp, n_mem, X_HEADS, -1)
    return (y_p.reshape(Bp, T, D), y_s.reshape(Bs, 1, D), *p_st, p_mem_k, p_mem_v, *s_st)
```

```python
import functools
import math

import jax
import jax.numpy as jnp
from jax import lax
from jax.experimental import pallas as pl
from jax.experimental.pallas import tpu as pltpu

F32 = jnp.float32
BF16 = jnp.bfloat16

EPS = 1e-6
ROPE_THETA = 10000.0
RET_HEADS = 4
GDN_HEADS = 16
X_HEADS = 4
POOL_WINDOWS = (2, 4, 8, 16)
POOL_BUF = max(POOL_WINDOWS) - 1
GDN_CONV = 4
FFN_CONV = 3
PAST_LEN = 16384

V7X_VMEM_BYTES = 64 * 1024 * 1024
VMEM_LIMIT = V7X_VMEM_BYTES - 8 * 1024 * 1024
LANES = 128
GDN_CHUNK = 64
GDN_SUB = 16
GDN_HB = 16
GDN_SOLVE = 8
NEG_BIG = -1e30


def _cparams(*sem):
    return pltpu.CompilerParams(dimension_semantics=sem, vmem_limit_bytes=VMEM_LIMIT)


def _dot(a, b):
    return jnp.dot(a, b, preferred_element_type=F32)


def _dot_nt(a, b):
    return lax.dot_general(a, b, (((1,), (1,)), ((), ())), preferred_element_type=F32)


def _dot_tn(a, b):
    return lax.dot_general(a, b, (((0,), (0,)), ((), ())), preferred_element_type=F32)


def _silu(x):
    return (0.5 * x) * (1.0 + jnp.tanh(0.5 * x))


def _blk(n, want):
    b = min(n, want)
    while n % b:
        b //= 2
    return b


def _rmsnorm_body(x_ref, g_ref, o_ref):
    x = x_ref[...]
    y = x * lax.rsqrt(jnp.mean(x * x, axis=-1, keepdims=True) + EPS)
    o_ref[...] = (y * g_ref[...]).astype(o_ref.dtype)


def _gain_spec(l, D):
    return pl.BlockSpec((None, 1, D), lambda i: (l, 0, 0))


def rmsnorm(x, g, l, out_dtype=BF16):
    M, D = x.shape
    bm = _blk(M, 256)
    return pl.pallas_call(
        _rmsnorm_body, grid=(M // bm,),
        in_specs=[pl.BlockSpec((bm, D), lambda i: (i, 0)), _gain_spec(l, D)],
        out_specs=pl.BlockSpec((bm, D), lambda i: (i, 0)),
        out_shape=jax.ShapeDtypeStruct((M, D), out_dtype),
        compiler_params=_cparams("parallel"), name="rmsnorm",
    )(x, g.reshape(-1, 1, D))


def _add_rmsnorm_body(x_ref, y_ref, g_ref, s_ref, o_ref):
    x = x_ref[...] + y_ref[...]
    s_ref[...] = x
    y = x * lax.rsqrt(jnp.mean(x * x, axis=-1, keepdims=True) + EPS)
    o_ref[...] = (y * g_ref[...]).astype(o_ref.dtype)


def add_rmsnorm(x, y, g, l, out_dtype=BF16):
    M, D = x.shape
    bm = _blk(M, 256)
    row = pl.BlockSpec((bm, D), lambda i: (i, 0))
    return pl.pallas_call(
        _add_rmsnorm_body, grid=(M // bm,),
        in_specs=[row, row, _gain_spec(l, D)],
        out_specs=[row, row],
        out_shape=[jax.ShapeDtypeStruct((M, D), F32), jax.ShapeDtypeStruct((M, D), out_dtype)],
        compiler_params=_cparams("parallel"), name="add_rmsnorm",
    )(x, y, g.reshape(-1, 1, D))


def _mm_body(*refs, widths, has_res):
    xs, w_ref = refs[:len(widths)], refs[len(widths)]
    o_ref = refs[-1]
    acc = refs[len(widths) + 1][...] if has_res else None
    k0 = 0
    for x_ref, kw in zip(xs, widths):
        d = _dot(x_ref[...], w_ref[k0:k0 + kw, :])
        acc = d if acc is None else acc + d
        k0 += kw
    o_ref[...] = acc.astype(o_ref.dtype)


def matmul(xs, w, l, res=None, out_dtype=F32, n_out=None, bm=1024, bn=1024):
    xs = list(xs) if isinstance(xs, (list, tuple)) else [xs]
    M = xs[0].shape[0]
    widths = tuple(x.shape[1] for x in xs)
    K, N = w.shape[1], n_out or w.shape[2]
    assert sum(widths) == K
    bm, bn = _blk(M, bm), _blk(N, bn)
    in_specs = [pl.BlockSpec((bm, kw), lambda i, j: (i, 0)) for kw in widths]
    in_specs.append(pl.BlockSpec((None, K, bn), lambda i, j: (l, 0, j)))
    args = xs + [w]
    if res is not None:
        in_specs.append(pl.BlockSpec((bm, bn), lambda i, j: (i, j)))
        args.append(res)
    return pl.pallas_call(
        functools.partial(_mm_body, widths=widths, has_res=res is not None),
        grid=(M // bm, N // bn), in_specs=in_specs,
        out_specs=pl.BlockSpec((bm, bn), lambda i, j: (i, j)),
        out_shape=jax.ShapeDtypeStruct((M, N), out_dtype),
        compiler_params=_cparams("parallel", "parallel"), name="matmul",
    )(*args)


def _ffn_act(a, u):
    return (0.5 * a * (1.0 + lax.erf(a * (2.0 ** -0.5))) * u).astype(BF16)


def _ffn_seq_body(h_ref, wg_ref, wu_ref, wd_ref, cw_ref, cb_ref, o_ref, fst_ref, tail_ref, act_ref, *,
                  bm, blocks_per_seq, nf):
    i = pl.program_id(0)
    f = pl.program_id(1)

    @pl.when(f == 0)
    def _():
        act_ref[...] = jnp.zeros(act_ref.shape, act_ref.dtype)
        o_ref[...] = jnp.zeros(o_ref.shape, o_ref.dtype)

    @pl.when((i % blocks_per_seq == 0) & (f < nf))
    def _():
        tail_ref[f] = jnp.zeros(tail_ref.shape[1:], F32)

    @pl.when(f == nf)
    def _():
        o_ref[...] += _dot(act_ref[...], wd_ref[...])

    @pl.when(f < nf)
    def _():
        o_ref[...] += _dot(act_ref[...], wd_ref[...])
        h = h_ref[...]
        g = _dot(h, wg_ref[...])
        u = _dot(h, wu_ref[...])
        prev = tail_ref[f]
        p2, p1 = prev[0:1, :], prev[1:2, :]
        row = lax.broadcasted_iota(jnp.int32, g.shape, 0)
        s1 = jnp.where(row == 0, p1, pltpu.roll(g, 1, 0))
        s2 = jnp.where(row == 0, p2, jnp.where(row == 1, p1, pltpu.roll(g, 2, 0)))
        cw = cw_ref[...]
        a = s2 * cw[0:1, :] + s1 * cw[1:2, :] + g * cw[2:3, :] + cb_ref[...]
        last2 = g[bm - 2:bm, :]
        tail_ref[f, 0:2, :] = last2
        fst_ref[0, f] = last2
        act_ref[...] = _ffn_act(a, u)


def ffn_seq(h, wg, wu, wd, cw, cb, l, nseq, bm=1024, bf=256):
    M, D = h.shape
    F = wg.shape[2]
    T = M // nseq
    bm = _blk(T, bm)
    nf = F // bf
    bps = T // bm
    body = functools.partial(_ffn_seq_body, bm=bm, blocks_per_seq=bps, nf=nf)
    cur = lambda f: jnp.minimum(f, nf - 1)
    y, fst = pl.pallas_call(
        body, grid=(M // bm, nf + 1),
        in_specs=[pl.BlockSpec((bm, D), lambda i, f: (i, 0)),
                  pl.BlockSpec((None, D, bf), lambda i, f: (l, 0, cur(f))),
                  pl.BlockSpec((None, D, bf), lambda i, f: (l, 0, cur(f))),
                  pl.BlockSpec((None, bf, D), lambda i, f: (l, jnp.maximum(f - 1, 0), 0)),
                  pl.BlockSpec((None, FFN_CONV, bf), lambda i, f: (l, 0, cur(f))),
                  pl.BlockSpec((None, 1, bf), lambda i, f: (l, 0, cur(f)))],
        out_specs=[pl.BlockSpec((bm, D), lambda i, f: (i, 0), pipeline_mode=pl.Buffered(1)),
                   pl.BlockSpec((1, nf, 2, bf), lambda i, f: (i // bps, 0, 0, 0))],
        out_shape=[jax.ShapeDtypeStruct((M, D), F32), jax.ShapeDtypeStruct((nseq, nf, 2, bf), F32)],
        scratch_shapes=[pltpu.VMEM((nf, 8, bf), F32), pltpu.VMEM((bm, bf), BF16)],
        compiler_params=_cparams("arbitrary", "arbitrary"), name="ffn_seq",
    )(h, wg, wu, wd, cw, cb.reshape(cb.shape[0], 1, F))
    return y, fst.transpose(0, 2, 1, 3).reshape(nseq, 2, F)


def _ffn_dec_body(h_ref, wg_ref, wu_ref, wd_ref, cw_ref, cb_ref, s0_ref, s1_ref, o_ref, g_ref):
    f = pl.program_id(0)
    h = h_ref[...]
    g = _dot(h, wg_ref[...])
    u = _dot(h, wu_ref[...])
    g_ref[...] = g
    cw = cw_ref[...]
    a = s0_ref[...] * cw[0:1, :] + s1_ref[...] * cw[1:2, :] + g * cw[2:3, :] + cb_ref[...]
    d = _dot(_ffn_act(a, u), wd_ref[...])

    @pl.when(f == 0)
    def _():
        o_ref[...] = d

    @pl.when(f != 0)
    def _():
        o_ref[...] += d


def ffn_dec(h, wg, wu, wd, cw, cb, st, l, bf=256):
    B, D = h.shape
    F = wg.shape[2]
    nf = F // bf
    st2 = st.reshape(st.shape[0], B, 2 * F)
    return pl.pallas_call(
        _ffn_dec_body, grid=(nf,),
        in_specs=[pl.BlockSpec((B, D), lambda f: (0, 0)),
                  pl.BlockSpec((None, D, bf), lambda f: (l, 0, f)),
                  pl.BlockSpec((None, D, bf), lambda f: (l, 0, f)),
                  pl.BlockSpec((None, bf, D), lambda f: (l, f, 0)),
                  pl.BlockSpec((None, FFN_CONV, bf), lambda f: (l, 0, f)),
                  pl.BlockSpec((None, 1, bf), lambda f: (l, 0, f)),
                  pl.BlockSpec((None, B, bf), lambda f: (l, 0, f)),
                  pl.BlockSpec((None, B, bf), lambda f: (l, 0, nf + f))],
        out_specs=[pl.BlockSpec((B, D), lambda f: (0, 0)), pl.BlockSpec((B, bf), lambda f: (0, f))],
        out_shape=[jax.ShapeDtypeStruct((B, D), F32), jax.ShapeDtypeStruct((B, F), F32)],
        compiler_params=_cparams("arbitrary"), name="ffn_dec",
    )(h, wg, wu, wd, cw, cb.reshape(cb.shape[0], 1, F), st2, st2)


def _log_gamma(h):
    return math.log(1.0 - 2.0 ** (-5.0 - h))


def _rope_tables(pos, half):
    inv = ROPE_THETA ** (-jnp.arange(half, dtype=F32) / half)
    ang = pos.astype(F32)[:, None] * inv[None, :]
    cos, sin = jnp.cos(ang), jnp.sin(ang)
    return jnp.concatenate([cos, cos], axis=-1), jnp.concatenate([-sin, sin], axis=-1)


def _rope(x, cosf, sins):
    return x * cosf + pltpu.roll(x, x.shape[-1] // 2, 1) * sins


def _ret_seq_body(q_ref, k_ref, v_ref, g_ref, cos_ref, sin_ref, o_ref, so_ref, s_ref, *, C, dk, dv):
    j = pl.program_id(1)

    @pl.when(j == 0)
    def _():
        s_ref[...] = jnp.zeros(s_ref.shape, F32)

    cosf, sins = cos_ref[...], sin_ref[...]
    ri = lax.broadcasted_iota(jnp.int32, (C, C), 0)
    ci = lax.broadcasted_iota(jnp.int32, (C, C), 1)
    idx = lax.broadcasted_iota(jnp.int32, (C, 1), 0).astype(F32)
    for h in range(RET_HEADS):
        lg = _log_gamma(h)
        q = _rope(q_ref[:, h * dk:(h + 1) * dk].astype(F32), cosf, sins)
        k = _rope(k_ref[:, h * dk:(h + 1) * dk].astype(F32), cosf, sins) * (dk ** -0.5)
        v = v_ref[:, h * dv:(h + 1) * dv]
        dmask = jnp.exp(jnp.where(ri >= ci, (ri - ci).astype(F32) * lg, NEG_BIG))
        scores = _dot_nt(q.astype(BF16), k.astype(BF16)) * dmask
        S = s_ref[h]
        qd = q * jnp.exp((idx + 1.0) * lg)
        o = _dot(scores.astype(BF16), v) + _dot(qd.astype(BF16), S.astype(BF16))
        kd = k * jnp.exp((C - 1.0 - idx) * lg)
        s_ref[h] = S * math.exp(C * lg) + _dot_tn(kd.astype(BF16), v)
        o = o * lax.rsqrt(jnp.mean(o * o, axis=-1, keepdims=True) + EPS)
        o_ref[:, h * dv:(h + 1) * dv] = (o * _silu(g_ref[:, h * dv:(h + 1) * dv].astype(F32))).astype(o_ref.dtype)

    @pl.when(j == pl.num_programs(1) - 1)
    def _():
        so_ref[0] = s_ref[...]


def retention_seq(z, B, T, dk, dv, C=256):
    C = _blk(T, C)
    nc = T // C
    qw, vw = RET_HEADS * dk, RET_HEADS * dv
    cosf, sins = _rope_tables(jnp.arange(T), dk // 2)
    body = functools.partial(_ret_seq_body, C=C, dk=dk, dv=dv)
    rows = lambda b, j: b * nc + j
    return pl.pallas_call(
        body, grid=(B, nc),
        in_specs=[pl.BlockSpec((C, qw), lambda b, j: (rows(b, j), 0)),
                  pl.BlockSpec((C, qw), lambda b, j: (rows(b, j), 1)),
                  pl.BlockSpec((C, vw), lambda b, j: (rows(b, j), (2 * qw) // vw)),
                  pl.BlockSpec((C, vw), lambda b, j: (rows(b, j), (2 * qw) // vw + 1)),
                  pl.BlockSpec((C, dk), lambda b, j: (j, 0)),
                  pl.BlockSpec((C, dk), lambda b, j: (j, 0))],
        out_specs=[pl.BlockSpec((C, vw), lambda b, j: (rows(b, j), 0)),
                   pl.BlockSpec((1, RET_HEADS, dk, dv), lambda b, j: (b, 0, 0, 0))],
        out_shape=[jax.ShapeDtypeStruct((B * T, vw), BF16), jax.ShapeDtypeStruct((B, RET_HEADS, dk, dv), F32)],
        scratch_shapes=[pltpu.VMEM((RET_HEADS, dk, dv), F32)],
        compiler_params=_cparams("arbitrary", "arbitrary"), name="retention_seq",
    )(z, z, z, z, cosf, sins)


def _row0(x, rows=8):
    r = lax.broadcasted_iota(jnp.int32, (rows, x.shape[1]), 0)
    return jnp.where(r == 0, jnp.broadcast_to(x, (rows, x.shape[1])), 0.0)


def _ret_dec_body(q_ref, k_ref, v_ref, g_ref, cos_ref, sin_ref, s_ref, acc_ref, o_ref, so_ref, qs_ref, *,
                  bb, dk, dv):
    cosf, sins = cos_ref[...], sin_ref[...]
    qs, ks = [], []
    for h in range(RET_HEADS):
        q = _rope(q_ref[:, h * dk:(h + 1) * dk].astype(F32), cosf, sins)
        k = _rope(k_ref[:, h * dk:(h + 1) * dk].astype(F32), cosf, sins) * (dk ** -0.5)
        qs.append(q)
        ks.append(k)
        for b in range(bb):
            r = _dot(_row0(q[b:b + 1]).astype(BF16), s_ref[b, h].astype(BF16))
            qs_ref[h, b:b + 1, :] = r[0:1]
    for h in range(RET_HEADS):
        gamma = math.exp(_log_gamma(h))
        v = v_ref[:, h * dv:(h + 1) * dv].astype(F32)
        o = gamma * qs_ref[h] + jnp.sum(qs[h] * ks[h], axis=-1, keepdims=True) * v
        o = o * lax.rsqrt(jnp.mean(o * o, axis=-1, keepdims=True) + EPS)
        gate = _silu(g_ref[:, h * dv:(h + 1) * dv].astype(F32))
        o_ref[:, h * dv:(h + 1) * dv] = (o * gate).astype(o_ref.dtype)
    for h in range(RET_HEADS):
        gamma = math.exp(_log_gamma(h))
        kT = ks[h].T
        v = v_ref[:, h * dv:(h + 1) * dv].astype(F32)
        for b in range(bb):
            kc = jnp.broadcast_to(kT[:, b:b + 1], (dk, dv))
            so_ref[b, h] = s_ref[b, h] * gamma + kc * v[b:b + 1]


def retention_dec(z, state, acc, l, dk, dv, bb=8):
    B = z.shape[0]
    bb = _blk(B, bb)
    qw, vw = RET_HEADS * dk, RET_HEADS * dv
    cosf, sins = _rope_tables(jnp.full((1,), PAST_LEN), dk // 2)
    body = functools.partial(_ret_dec_body, bb=bb, dk=dk, dv=dv)
    st = pl.BlockSpec((None, bb, RET_HEADS, dk, dv), lambda i: (l, i, 0, 0, 0))
    return pl.pallas_call(
        body, grid=(B // bb,),
        in_specs=[pl.BlockSpec((bb, qw), lambda i: (i, 0)),
                  pl.BlockSpec((bb, qw), lambda i: (i, 1)),
                  pl.BlockSpec((bb, vw), lambda i: (i, (2 * qw) // vw)),
                  pl.BlockSpec((bb, vw), lambda i: (i, (2 * qw) // vw + 1)),
                  pl.BlockSpec((1, dk), lambda i: (0, 0)),
                  pl.BlockSpec((1, dk), lambda i: (0, 0)), st, pl.BlockSpec(memory_space=pl.ANY)],
        out_specs=[pl.BlockSpec((bb, vw), lambda i: (i, 0)), st],
        out_shape=[jax.ShapeDtypeStruct((B, vw), BF16), jax.ShapeDtypeStruct(state.shape, F32)],
        input_output_aliases={7: 1},
        scratch_shapes=[pltpu.VMEM((RET_HEADS, bb, dv), F32)],
        compiler_params=_cparams("parallel"), name="retention_dec",
    )(z, z, z, z, cosf, sins, state, acc)


def _pool_seq_body(u_ref, w_ref, ls_ref, o_ref, ext_ref, *, R, gw):
    j = pl.program_id(1)

    @pl.when(j == 0)
    def _():
        ext_ref[0:16, :] = jnp.zeros((16, ext_ref.shape[1]), F32)

    ext_ref[16:16 + R, :] = u_ref[...].astype(F32)
    pos = j * R + lax.broadcasted_iota(jnp.int32, (R, 1), 0)
    for gi, win in enumerate(POOL_WINDOWS):
        sl = slice(gi * gw, (gi + 1) * gw)
        cur = ext_ref[16:16 + R, sl]
        acc = cur
        for s in range(1, win):
            acc = acc + ext_ref[16 - s:16 - s + R, sl]
        cnt = jnp.minimum(pos + 1, win).astype(F32)
        d = acc / cnt - cur
        o_ref[:, sl] = (_dot(d.astype(BF16), w_ref[gi]) * ls_ref[:, sl]).astype(o_ref.dtype)
    ext_ref[0:16, :] = ext_ref[R:R + 16, :]


def pool_seq(z, w_pool, ls, l, B, T, col_block, R=256):
    R = _blk(T, R)
    nr = T // R
    gw = w_pool.shape[-1]
    W = gw * len(POOL_WINDOWS)
    body = functools.partial(_pool_seq_body, R=R, gw=gw)
    return pl.pallas_call(
        body, grid=(B, nr),
        in_specs=[pl.BlockSpec((R, W), lambda b, j: (b * nr + j, col_block)),
                  pl.BlockSpec((None,) + w_pool.shape[1:], lambda b, j: (l, 0, 0, 0)),
                  pl.BlockSpec((None, 1, W), lambda b, j: (l, 0, 0))],
        out_specs=pl.BlockSpec((R, W), lambda b, j: (b * nr + j, 0)),
        out_shape=jax.ShapeDtypeStruct((B * T, W), BF16),
        scratch_shapes=[pltpu.VMEM((16 + R, W), F32)],
        compiler_params=_cparams("arbitrary", "arbitrary"), name="pool_seq",
    )(z, w_pool, ls.reshape(-1, 1, W))


def _pool_dec_body(u_ref, st_ref, w_ref, ls_ref, o_ref, *, gw, W):
    for gi, win in enumerate(POOL_WINDOWS):
        sl = slice(gi * gw, (gi + 1) * gw)
        cur = u_ref[:, sl].astype(F32)
        acc = cur
        for s in range(1, win):
            r = POOL_BUF - s
            acc = acc + st_ref[:, r * W + gi * gw:r * W + (gi + 1) * gw]
        cnt = float(min(PAST_LEN + 1, win))
        d = acc / cnt - cur
        o_ref[:, sl] = (_dot(d.astype(BF16), w_ref[gi]) * ls_ref[:, sl]).astype(o_ref.dtype)


def pool_dec(z, state, w_pool, ls, l, col_block, bb=64):
    B = z.shape[0]
    bb = _blk(B, bb)
    gw = w_pool.shape[-1]
    W = gw * len(POOL_WINDOWS)
    body = functools.partial(_pool_dec_body, gw=gw, W=W)
    return pl.pallas_call(
        body, grid=(B // bb,),
        in_specs=[pl.BlockSpec((bb, W), lambda i: (i, col_block)),
                  pl.BlockSpec((None, bb, POOL_BUF * W), lambda i: (l, i, 0)),
                  pl.BlockSpec((None,) + w_pool.shape[1:], lambda i: (l, 0, 0, 0)),
                  pl.BlockSpec((None, 1, W), lambda i: (l, 0, 0))],
        out_specs=pl.BlockSpec((bb, W), lambda i: (i, 0)),
        out_shape=jax.ShapeDtypeStruct((B, W), BF16),
        compiler_params=_cparams("parallel"), name="pool_dec",
    )(z, state.reshape(-1, B, POOL_BUF * W), w_pool, ls.reshape(-1, 1, W))


def _l2norm(x):
    return x * lax.rsqrt(jnp.sum(x * x, axis=-1, keepdims=True) + EPS)


def _softplus(x):
    return jnp.maximum(x, 0.0) + jnp.log1p(jnp.exp(-jnp.abs(x)))


def _sigmoid(x):
    return 1.0 / (1.0 + jnp.exp(-x))


def _solve_unit_lower(Ls, R, W):
    c = Ls[0].shape[0]
    n = len(Ls)
    half = GDN_SUB // 2
    blocks = []
    for a in range(c // GDN_SUB):
        r0 = a * GDN_SUB
        Rb = R[r0:r0 + GDN_SUB, :]
        if a:
            Xp = jnp.concatenate(blocks + [jnp.zeros((c - r0, n * W), F32)], axis=0).astype(BF16)
            Rb = Rb - jnp.concatenate(
                [_dot(Ls[h][r0:r0 + GDN_SUB, :].astype(BF16), Xp[:, h * W:(h + 1) * W]) for h in range(n)], axis=1)
        top, bot = Rb[:half], Rb[half:]
        for jj in range(GDN_SUB - 1):
            coef = jnp.concatenate(
                [jnp.broadcast_to(Ls[h][r0:r0 + GDN_SUB, r0 + jj:r0 + jj + 1], (GDN_SUB, W)) for h in range(n)], axis=1)
            row = top[jj:jj + 1] if jj < half else bot[jj - half:jj - half + 1]
            if jj < half - 1:
                top = top - coef[:half] * row
            bot = bot - coef[half:] * row
        blocks.append(jnp.concatenate([top, bot], axis=0))
    return jnp.concatenate(blocks, axis=0)


def _gdn_seq_body(q_ref, k_ref, v_ref, z_ref, wq_ref, wk_ref, wv_ref, ba_ref, al_ref, dt_ref, nw_ref,
                  o_ref, so_ref, s_ref, eq_ref, ek_ref, ev_ref, cq_ref, ck_ref, cv_ref, *, R, dk, dv):
    r = pl.program_id(2)
    c = GDN_CHUNK
    hb = GDN_HB

    @pl.when(r == 0)
    def _():
        s_ref[...] = jnp.zeros(s_ref.shape, F32)
        for e in (eq_ref, ek_ref, ev_ref):
            e[...] = jnp.zeros(e.shape, F32)

    row8 = lax.broadcasted_iota(jnp.int32, eq_ref.shape, 0)
    for x_ref, w_ref, e_ref, c_ref in ((q_ref, wq_ref, eq_ref, cq_ref), (k_ref, wk_ref, ek_ref, ck_ref),
                                       (v_ref, wv_ref, ev_ref, cv_ref)):
        xf = x_ref[...].astype(F32)
        prev = e_ref[...]
        w = w_ref[...]
        acc = xf * w[GDN_CONV - 1:GDN_CONV, :]
        for s in range(1, GDN_CONV):
            xs = pltpu.roll(xf, s, 0)
            top = jnp.where(row8 < s, pltpu.roll(prev, s, 0), xs[0:8])
            xs = jnp.concatenate([top, xs[8:]], axis=0)
            acc = acc + xs * w[GDN_CONV - 1 - s:GDN_CONV - s, :]
        c_ref[...] = _silu(acc)
        e_ref[...] = xf[R - 8:R]

    ri = lax.broadcasted_iota(jnp.int32, (c, c), 0)
    ci = lax.broadcasted_iota(jnp.int32, (c, c), 1)
    tril = (ri >= ci).astype(F32)
    triu = (ri <= ci).astype(F32)
    neg_a = -jnp.exp(al_ref[0])
    dtb = dt_ref[0]
    nw = nw_ref[...]

    def chunk(ic, carry):
        r0 = pl.multiple_of(ic * c, c)
        rows = pl.ds(r0, c)
        x = ba_ref[rows, :]
        beta_all = _sigmoid(x)
        g_all = neg_a * _softplus(x + dtb)
        b_col = jnp.dot(tril, g_all, precision=lax.Precision.HIGHEST, preferred_element_type=F32)
        b_row = lax.dot_general(g_all, triu, (((0,), (0,)), ((), ())), precision=lax.Precision.HIGHEST,
                                preferred_element_type=F32)
        W = dv + dk
        qs, ks, Ls, attns, ebs, bcs, rhs, gates = [], [], [], [], [], [], [], []
        for hh in range(hb):
            hs = slice(hh * dk, (hh + 1) * dk)
            q = _l2norm(cq_ref[rows, hs]) * (dk ** -0.5)
            k = _l2norm(ck_ref[rows, hs])
            v = cv_ref[rows, hh * dv:(hh + 1) * dv]
            gates.append(_silu(z_ref[rows, hh * dv:(hh + 1) * dv].astype(F32)))
            bc = b_col[:, hb + hh:hb + hh + 1]
            br = b_row[hb + hh:hb + hh + 1, :]
            beta = beta_all[:, hh:hh + 1]
            decay = jnp.exp(jnp.where(ri >= ci, bc - br, NEG_BIG))
            kb = k * beta
            kbf = k.astype(BF16)
            Ls.append(jnp.where(ri > ci, _dot_nt(kb.astype(BF16), kbf) * decay, 0.0))
            attns.append((_dot_nt(q.astype(BF16), kbf) * decay).astype(BF16))
            eb = jnp.exp(bc)
            rhs.append(jnp.concatenate([v * beta, kb * eb], axis=1))
            qs.append(q)
            ks.append(k)
            ebs.append(eb)
            bcs.append(bc)
        uws = [_solve_unit_lower(Ls[g0:g0 + GDN_SOLVE], jnp.concatenate(rhs[g0:g0 + GDN_SOLVE], axis=1), W)
               for g0 in range(0, hb, GDN_SOLVE)]
        outs, states = [], []
        for hh in range(hb):
            uw = uws[hh // GDN_SOLVE]
            o0 = (hh % GDN_SOLVE) * W
            u, w = uw[:, o0:o0 + dv], uw[:, o0 + dv:o0 + W]
            q, k, bc = qs[hh], ks[hh], bcs[hh]
            S = s_ref[hh]
            Sb = S.astype(BF16)
            v_new = u - _dot(w.astype(BF16), Sb)
            vnb = v_new.astype(BF16)
            o = _dot((q * ebs[hh]).astype(BF16), Sb) + _dot(attns[hh], vnb)
            bl = bc[c - 1:c, :]
            states.append(S * jnp.exp(bl) + _dot_tn((k * jnp.exp(bl - bc)).astype(BF16), vnb))
            o = o * lax.rsqrt(jnp.mean(o * o, axis=-1, keepdims=True) + EPS) * nw
            outs.append((o * gates[hh]).astype(o_ref.dtype))
        for hh in range(hb):
            s_ref[hh] = states[hh]
            o_ref[rows, hh * dv:(hh + 1) * dv] = outs[hh]
        return carry

    lax.fori_loop(0, R // c, chunk, 0)

    @pl.when(r == pl.num_programs(2) - 1)
    def _():
        so_ref[0] = s_ref[...]


def _group_lanes(t):
    hb, HG = GDN_HB, GDN_HEADS // GDN_HB
    t = t.reshape(t.shape[:-1] + (HG, hb))
    t = jnp.pad(t, [(0, 0)] * (t.ndim - 1) + [(hb, LANES - 2 * hb)])
    return t.reshape(t.shape[:-2] + (HG * LANES,))


def gdn_seq(z, zba, w_gconv, a_log, dt_bias, norm_w, l, B, T, dk, dv, col0, R=256):
    H, hb = GDN_HEADS, GDN_HB
    HG = H // hb
    R = _blk(T, R)
    nr = T // R
    M = B * T
    bw = hb * dk
    cb0 = col0 // bw
    per = (H * dk) // bw
    body = functools.partial(_gdn_seq_body, R=R, dk=dk, dv=dv)
    rows = lambda b, g, r: b * nr + r
    zspec = lambda off: pl.BlockSpec((R, bw), lambda b, g, r: (rows(b, g, r), cb0 + off + g))
    wspec = lambda off: pl.BlockSpec((None, GDN_CONV, bw), lambda b, g, r: (l, 0, off + g))
    pspec = pl.BlockSpec((1, 1, LANES), lambda b, g, r: (l * HG + g, 0, 0))
    return pl.pallas_call(
        body, grid=(B, HG, nr),
        in_specs=[zspec(0), zspec(per), zspec(2 * per), zspec(3 * per),
                  wspec(0), wspec(per), wspec(2 * per),
                  pl.BlockSpec((R, LANES), lambda b, g, r: (rows(b, g, r), g)), pspec, pspec,
                  pl.BlockSpec((None, 1, dv), lambda b, g, r: (l, 0, 0))],
        out_specs=[pl.BlockSpec((R, bw), lambda b, g, r: (rows(b, g, r), g)),
                   pl.BlockSpec((1, hb, dk, dv), lambda b, g, r: (b, g, 0, 0))],
        out_shape=[jax.ShapeDtypeStruct((M, H * dv), BF16), jax.ShapeDtypeStruct((B, H, dk, dv), F32)],
        scratch_shapes=[pltpu.VMEM((hb, dk, dv), F32)] + [pltpu.VMEM((8, bw), F32)] * 3
                       + [pltpu.VMEM((R, bw), F32)] * 3,
        compiler_params=_cparams("arbitrary", "arbitrary", "arbitrary"), name="gdn_seq",
    )(z, z, z, z, w_gconv, w_gconv, w_gconv, zba,
      _group_lanes(a_log).reshape(-1, 1, LANES), _group_lanes(dt_bias).reshape(-1, 1, LANES),
      norm_w.reshape(-1, 1, dv))


def _gdn_dec_body(xq_ref, xk_ref, xv_ref, z_ref, cs_ref, w_ref, b_ref, a_ref, al_ref, dt_ref, nw_ref, s_ref,
                  acc_ref, o_ref, so_ref, cu_ref, ks_ref, qs_ref, *, bb, dk, dv, CH):
    H = GDN_HEADS
    GW = CH // 3
    for gi, x_ref in enumerate((xq_ref, xk_ref, xv_ref)):
        cs = slice(gi * GW, (gi + 1) * GW)
        acc = x_ref[...].astype(F32) * w_ref[GDN_CONV - 1:GDN_CONV, cs]
        for i in range(GDN_CONV - 1):
            acc = acc + cs_ref[:, i * CH + gi * GW:i * CH + (gi + 1) * GW] * w_ref[i:i + 1, cs]
        cu_ref[:, cs] = _silu(acc)
    beta = _sigmoid(b_ref[...])
    eg = jnp.exp(-jnp.exp(al_ref[...]) * _softplus(a_ref[...] + dt_ref[...]))
    nw = nw_ref[...]
    qo, ko, vo = 0, H * dk, 2 * H * dk
    ks, qs = [], []
    for h in range(H):
        q = _l2norm(cu_ref[:, qo + h * dk:qo + (h + 1) * dk]) * (dk ** -0.5)
        k = _l2norm(cu_ref[:, ko + h * dk:ko + (h + 1) * dk])
        ks.append(k)
        qs.append(q)
        for b in range(bb):
            kq = jnp.concatenate([k[b:b + 1], q[b:b + 1], jnp.zeros((6, dk), F32)], axis=0)
            r = _dot(kq.astype(BF16), s_ref[b, h].astype(BF16))
            ks_ref[h, b:b + 1, :] = r[0:1]
            qs_ref[h, b:b + 1, :] = r[1:2]
    vns = []
    for h in range(H):
        q, k = qs[h], ks[h]
        v = cu_ref[:, vo + h * dv:vo + (h + 1) * dv]
        e = eg[:, h:h + 1]
        v_new = beta[:, h:h + 1] * (v - e * ks_ref[h])
        o = e * qs_ref[h] + jnp.sum(q * k, axis=-1, keepdims=True) * v_new
        o = o * lax.rsqrt(jnp.mean(o * o, axis=-1, keepdims=True) + EPS) * nw
        gate = _silu(z_ref[:, h * dv:(h + 1) * dv].astype(F32))
        o_ref[:, h * dv:(h + 1) * dv] = (o * gate).astype(o_ref.dtype)
        vns.append(v_new)
    for h in range(H):
        kT = ks[h].T
        for b in range(bb):
            kc = jnp.broadcast_to(kT[:, b:b + 1], (dk, dv))
            so_ref[b, h] = s_ref[b, h] * eg[b:b + 1, h:h + 1] + kc * vns[h][b:b + 1]


def gdn_dec(z, zb, za, conv_state, w_gconv, a_log, dt_bias, norm_w, state, acc, l, dk, dv, col0, bb=8):
    B = z.shape[0]
    H = GDN_HEADS
    bb = _blk(B, bb)
    CH = 3 * H * dk
    GW = H * dk
    cb0 = col0 // GW
    body = functools.partial(_gdn_dec_body, bb=bb, dk=dk, dv=dv, CH=CH)
    st = pl.BlockSpec((None, bb, H, dk, dv), lambda i: (l, i, 0, 0, 0))
    hrow = pl.BlockSpec((bb, H), lambda i: (i, 0))
    hpar = pl.BlockSpec((None, 1, H), lambda i: (l, 0, 0))
    zspec = lambda off: pl.BlockSpec((bb, GW), lambda i: (i, cb0 + off))
    return pl.pallas_call(
        body, grid=(B // bb,),
        in_specs=[zspec(0), zspec(1), zspec(2), zspec(3),
                  pl.BlockSpec((None, bb, (GDN_CONV - 1) * CH), lambda i: (l, i, 0)),
                  pl.BlockSpec((None, GDN_CONV, CH), lambda i: (l, 0, 0)),
                  hrow, hrow, hpar, hpar,
                  pl.BlockSpec((None, 1, dv), lambda i: (l, 0, 0)), st, pl.BlockSpec(memory_space=pl.ANY)],
        out_specs=[pl.BlockSpec((bb, H * dv), lambda i: (i, 0)), st],
        out_shape=[jax.ShapeDtypeStruct((B, H * dv), BF16), jax.ShapeDtypeStruct(state.shape, F32)],
        input_output_aliases={12: 1},
        scratch_shapes=[pltpu.VMEM((bb, CH), F32), pltpu.VMEM((H, bb, dv), F32), pltpu.VMEM((H, bb, dv), F32)],
        compiler_params=_cparams("parallel"), name="gdn_dec",
    )(z, z, z, z, conv_state.reshape(-1, B, (GDN_CONV - 1) * CH), w_gconv, zb, za,
      a_log.reshape(-1, 1, H), dt_bias.reshape(-1, 1, H), norm_w.reshape(-1, 1, dv), state, acc)


def _softmax_rows(s):
    m = jnp.max(s, axis=-1, keepdims=True)
    e = jnp.exp(s - m)
    return e / jnp.sum(e, axis=-1, keepdims=True)


def _xattn_seq_body(q_ref, k_ref, v_ref, o_ref, *, hd):
    for h in range(X_HEADS):
        sl = slice(h * hd, (h + 1) * hd)
        s = _dot_nt(q_ref[:, sl], k_ref[0, :, sl].astype(BF16)) * (hd ** -0.5)
        a = _softmax_rows(s)
        o_ref[:, sl] = _dot(a.astype(BF16), v_ref[0, :, sl].astype(BF16)).astype(o_ref.dtype)


def xattn_seq(q, mk, mv, B, T, bq=512):
    W = q.shape[1]
    hd = W // X_HEADS
    bq = _blk(T, bq)
    nq = T // bq
    mem = pl.BlockSpec((1,) + mk.shape[1:], lambda b, j: (b, 0, 0))
    return pl.pallas_call(
        functools.partial(_xattn_seq_body, hd=hd), grid=(B, nq),
        in_specs=[pl.BlockSpec((bq, W), lambda b, j: (b * nq + j, 0)), mem, mem],
        out_specs=pl.BlockSpec((bq, W), lambda b, j: (b * nq + j, 0)),
        out_shape=jax.ShapeDtypeStruct((B * T, W), BF16),
        compiler_params=_cparams("parallel", "parallel"), name="xattn_seq",
    )(q, mk, mv)


def _xattn_dec_body(q_ref, k_hbm, v_hbm, o_ref, kbuf, vbuf, sem, *, l, bb, hd, nsteps):
    i = pl.program_id(0)
    W = X_HEADS * hd

    def copies(step, slot):
        out = []
        for t, (src, dst) in enumerate(((k_hbm, kbuf), (v_hbm, vbuf))):
            for h in range(X_HEADS):
                out.append(pltpu.make_async_copy(src.at[l, pl.ds(step * bb, bb), :, h, :],
                                                 dst.at[slot, :, :, pl.ds(h * hd, hd)], sem.at[t, slot, h]))
        return out

    slot = i % 2

    @pl.when(i == 0)
    def _():
        for cp in copies(0, 0):
            cp.start()

    @pl.when(i + 1 < nsteps)
    def _():
        for cp in copies(i + 1, 1 - slot):
            cp.start()

    for cp in copies(i, slot):
        cp.wait()

    q = q_ref[...]
    row = lax.broadcasted_iota(jnp.int32, (8, W), 0)
    head = lax.broadcasted_iota(jnp.int32, (8, W), 1) // hd
    for b in range(bb):
        qd = jnp.where(row == head, jnp.broadcast_to(q[b:b + 1].astype(F32), (8, W)), 0.0).astype(BF16)
        s = _dot_nt(qd, kbuf[slot, b].astype(BF16)) * (hd ** -0.5)
        a = _softmax_rows(s)
        o = _dot(a.astype(BF16), vbuf[slot, b].astype(BF16))
        o_ref[b:b + 1, :] = jnp.sum(jnp.where(row == head, o, 0.0), axis=0, keepdims=True).astype(o_ref.dtype)


def xattn_dec(q, mk, mv, l, bb=8):
    B, W = q.shape
    hd = W // X_HEADS
    n_mem = mk.shape[2]
    bb = _blk(B, bb)
    nsteps = B // bb
    hbm = pl.BlockSpec(memory_space=pl.ANY)
    return pl.pallas_call(
        functools.partial(_xattn_dec_body, l=l, bb=bb, hd=hd, nsteps=nsteps), grid=(nsteps,),
        in_specs=[pl.BlockSpec((bb, W), lambda i: (i, 0)), hbm, hbm],
        out_specs=pl.BlockSpec((bb, W), lambda i: (i, 0)),
        out_shape=jax.ShapeDtypeStruct((B, W), BF16),
        scratch_shapes=[pltpu.VMEM((2, bb, n_mem, W), F32), pltpu.VMEM((2, bb, n_mem, W), F32),
                        pltpu.SemaphoreType.DMA((2, 2, X_HEADS))],
        compiler_params=_cparams("arbitrary"), name="xattn_dec",
    )(q, mk, mv)


def _run_trunk(x, nseq, T, states, mem_k, mem_v, W, depth):
    D = x.shape[1]
    dk_r, dv_r = D // 32, D // 16
    dk_g = dv_g = D // 32
    n_main = W["w_in"].shape[2] - 2 * GDN_HEADS
    pool_w = W["w_pool"].shape[-1] * len(POOL_WINDOWS)
    off_pu = 2 * RET_HEADS * dk_r + 2 * RET_HEADS * dv_r
    off_c = off_pu + pool_w
    ch = 3 * GDN_HEADS * dk_g
    gdn_par = (W["w_gconv"], W["gdn_a_log"], W["gdn_dt_bias"], W["gdn_norm"])
    ffn_par = (W["w_gate"], W["w_up"], W["w_down"], W["w_fconv"], W["b_fconv"])
    outs = [[] for _ in range(5)]
    if states is not None:
        acc_ret, acc_gdn = lax.empty(states[0].shape, F32), lax.empty(states[2].shape, F32)
    hn = rmsnorm(x, W["norm_mix"], 0)
    for l in range(depth):
        z = matmul(hn, W["w_in"], l, out_dtype=BF16, n_out=n_main)
        zba = matmul(hn, W["w_ba"], l, out_dtype=F32)
        if states is None:
            ro, s_ret = retention_seq(z, nseq, T, dk_r, dv_r)
            po = pool_seq(z, W["w_pool"], W["ls_pool"], l, nseq, T, off_pu // pool_w)
            co, s_gdn = gdn_seq(z, zba, *gdn_par, l, nseq, T, dk_g, dv_g, off_c)
            z3 = z.reshape(nseq, T, n_main)
            s_pool = z3[:, T - POOL_BUF:, off_pu:off_pu + pool_w].astype(F32)
            s_gconv = z3[:, T - (GDN_CONV - 1):, off_c:off_c + ch].astype(F32)
        else:
            st_ret, st_pool, st_gdn, st_gconv, st_fconv = states
            zg = zba.reshape(zba.shape[0], GDN_HEADS // GDN_HB, LANES)
            zb = zg[:, :, :GDN_HB].reshape(-1, GDN_HEADS)
            za = zg[:, :, GDN_HB:2 * GDN_HB].reshape(-1, GDN_HEADS)
            ro, acc_ret = retention_dec(z, st_ret, acc_ret, l, dk_r, dv_r)
            po = pool_dec(z, st_pool, W["w_pool"], W["ls_pool"], l, off_pu // pool_w)
            co, acc_gdn = gdn_dec(z, zb, za, st_gconv, *gdn_par, st_gdn, acc_gdn, l, dk_g, dv_g, off_c)
            s_ret = s_gdn = None
            s_pool = jnp.concatenate([st_pool[l, :, 1:], z[:, None, off_pu:off_pu + pool_w].astype(F32)], axis=1)
            s_gconv = jnp.concatenate([st_gconv[l, :, 1:], z[:, None, off_c:off_c + ch].astype(F32)], axis=1)
        x = matmul([ro, po, co], W["w_out"], l, res=x)
        hx = rmsnorm(x, W["norm_x"], l)
        q = matmul(hx, W["w_xq"], l, out_dtype=BF16)
        ao = xattn_seq(q, mem_k[l], mem_v[l], nseq, T) if states is None else xattn_dec(q, mem_k, mem_v, l)
        x = matmul(ao, W["w_xo"], l, res=x)
        hf = rmsnorm(x, W["norm_ffn"], l)
        if states is None:
            y, s_fconv = ffn_seq(hf, *ffn_par, l, nseq)
        else:
            y, g_new = ffn_dec(hf, *ffn_par, st_fconv, l)
            s_fconv = jnp.concatenate([st_fconv[l, :, 1:], g_new[:, None, :]], axis=1)
        if l + 1 < depth:
            x, hn = add_rmsnorm(x, y, W["norm_mix"], l + 1)
        else:
            x, hn = add_rmsnorm(x, y, W["norm_f"], 0, out_dtype=F32)
        for lst, s in zip(outs, (s_ret, s_pool, s_gdn, s_gconv, s_fconv)):
            lst.append(s)
    if states is None:
        return hn, [jnp.stack(s) for s in outs]
    return hn, [acc_ret, jnp.stack(outs[1]), acc_gdn, jnp.stack(outs[3]), jnp.stack(outs[4])]


def kernel(x_prompt, x_sample, state_ret, state_pool, state_gdn, state_gdn_conv, state_ffn_conv, cache_mem_k, cache_mem_v, mem_prompt, norm_mix, w_in, w_pool, ls_pool, w_gconv, gdn_a_log, gdn_dt_bias, gdn_norm, w_out, norm_x, norm_mem, w_xq, w_xk, w_xv, w_xo, norm_ffn, w_gate, w_up, w_fconv, b_fconv, w_down, norm_f):
    depth = w_in.shape[0]
    Bp, T, D = x_prompt.shape
    Bs = x_sample.shape[0]
    assert x_sample.shape[1] == 1 and D % (32 * LANES) == 0
    n_in = w_in.shape[2]
    n_main = n_in - 2 * GDN_HEADS
    hg = GDN_HEADS // GDN_HB
    w_b = w_in[:, :, n_main:n_main + GDN_HEADS].reshape(depth, D, hg, GDN_HB)
    w_a = w_in[:, :, n_main + GDN_HEADS:].reshape(depth, D, hg, GDN_HB)
    w_ba = jnp.pad(jnp.concatenate([w_b, w_a], axis=-1), ((0, 0), (0, 0), (0, 0), (0, LANES - 2 * GDN_HB)))
    w_ba = w_ba.reshape(depth, D, hg * LANES)
    W = dict(norm_mix=norm_mix, norm_x=norm_x, norm_ffn=norm_ffn, norm_f=norm_f.reshape(1, D),
             w_in=w_in.astype(BF16), w_ba=w_ba.astype(BF16),
             w_pool=w_pool.astype(BF16), ls_pool=ls_pool, w_gconv=w_gconv,
             gdn_a_log=gdn_a_log, gdn_dt_bias=gdn_dt_bias, gdn_norm=gdn_norm,
             w_out=w_out.astype(BF16), w_xq=w_xq.astype(BF16), w_xo=w_xo.astype(BF16),
             w_gate=w_gate.astype(BF16), w_up=w_up.astype(BF16), w_down=w_down.astype(BF16),
             w_fconv=w_fconv, b_fconv=b_fconv)
    n_mem = mem_prompt.shape[1]
    memf = mem_prompt.reshape(Bp * n_mem, D)
    w_xk16, w_xv16 = w_xk.astype(BF16), w_xv.astype(BF16)
    pk, pv = [], []
    for l in range(depth):
        mn = rmsnorm(memf, norm_mem, l)
        pk.append(matmul(mn, w_xk16, l).reshape(Bp, n_mem, -1))
        pv.append(matmul(mn, w_xv16, l).reshape(Bp, n_mem, -1))
    y_p, p_st = _run_trunk(x_prompt.reshape(Bp * T, D), Bp, T, None, pk, pv, W, depth)
    y_s, s_st = _run_trunk(x_sample.reshape(Bs, D), Bs, 1,
                           (state_ret, state_pool, state_gdn, state_gdn_conv, state_ffn_conv),
                           cache_mem_k, cache_mem_v, W, depth)
    p_mem_k = jnp.stack(pk).reshape(depth, Bp, n_mem, X_HEADS, -1)
    p_mem_v = jnp.stack(pv).reshape(depth, Bp, n_mem, X_HEADS, -1)
    return (y_p.reshape(Bp, T, D), y_s.reshape(Bs, 1, D), *p_st, p_mem_k, p_mem_v, *s_st)
```

```python
import functools
import math

import jax
import jax.numpy as jnp
from jax import lax
from jax.experimental import pallas as pl
from jax.experimental.pallas import tpu as pltpu

F32 = jnp.float32
BF16 = jnp.bfloat16

EPS = 1e-6
ROPE_THETA = 10000.0
RET_HEADS = 4
GDN_HEADS = 16
X_HEADS = 4
POOL_WINDOWS = (2, 4, 8, 16)
POOL_BUF = max(POOL_WINDOWS) - 1
GDN_CONV = 4
FFN_CONV = 3
PAST_LEN = 16384

V7X_VMEM_BYTES = 64 * 1024 * 1024
VMEM_LIMIT = V7X_VMEM_BYTES - 8 * 1024 * 1024
LANES = 128
GDN_CHUNK = 64
GDN_SUB = 16
GDN_HB = 16
GDN_SOLVE = 8
NEG_BIG = -1e30


def _cparams(*sem):
    return pltpu.CompilerParams(dimension_semantics=sem, vmem_limit_bytes=VMEM_LIMIT)


def _dot(a, b):
    return jnp.dot(a, b, preferred_element_type=F32)


def _dot_nt(a, b):
    return lax.dot_general(a, b, (((1,), (1,)), ((), ())), preferred_element_type=F32)


def _dot_tn(a, b):
    return lax.dot_general(a, b, (((0,), (0,)), ((), ())), preferred_element_type=F32)


def _silu(x):
    return (0.5 * x) * (1.0 + jnp.tanh(0.5 * x))


def _blk(n, want):
    for step in (16, 8):
        for b in range(min(n, want) // step * step, 0, -step):
            if n % b == 0:
                return b
    return n


def _rmsnorm_body(x_ref, g_ref, o_ref):
    x = x_ref[...]
    y = x * lax.rsqrt(jnp.mean(x * x, axis=-1, keepdims=True) + EPS)
    o_ref[...] = (y * g_ref[...]).astype(o_ref.dtype)


def _gain_spec(l, D):
    return pl.BlockSpec((None, 1, D), lambda i: (l, 0, 0))


def rmsnorm(x, g, l, out_dtype=BF16):
    M, D = x.shape
    bm = _blk(M, 256)
    return pl.pallas_call(
        _rmsnorm_body, grid=(M // bm,),
        in_specs=[pl.BlockSpec((bm, D), lambda i: (i, 0)), _gain_spec(l, D)],
        out_specs=pl.BlockSpec((bm, D), lambda i: (i, 0)),
        out_shape=jax.ShapeDtypeStruct((M, D), out_dtype),
        compiler_params=_cparams("parallel"), name="rmsnorm",
    )(x, g.reshape(-1, 1, D))


def _add_rmsnorm_body(x_ref, y_ref, g_ref, s_ref, o_ref):
    x = x_ref[...] + y_ref[...]
    s_ref[...] = x
    y = x * lax.rsqrt(jnp.mean(x * x, axis=-1, keepdims=True) + EPS)
    o_ref[...] = (y * g_ref[...]).astype(o_ref.dtype)


def add_rmsnorm(x, y, g, l, out_dtype=BF16):
    M, D = x.shape
    bm = _blk(M, 256)
    row = pl.BlockSpec((bm, D), lambda i: (i, 0))
    return pl.pallas_call(
        _add_rmsnorm_body, grid=(M // bm,),
        in_specs=[row, row, _gain_spec(l, D)],
        out_specs=[row, row],
        out_shape=[jax.ShapeDtypeStruct((M, D), F32), jax.ShapeDtypeStruct((M, D), out_dtype)],
        compiler_params=_cparams("parallel"), name="add_rmsnorm",
    )(x, y, g.reshape(-1, 1, D))


def _mm_body(*refs, widths, has_res):
    xs, w_ref = refs[:len(widths)], refs[len(widths)]
    o_ref = refs[-1]
    acc = refs[len(widths) + 1][...] if has_res else None
    k0 = 0
    for x_ref, kw in zip(xs, widths):
        d = _dot(x_ref[...], w_ref[k0:k0 + kw, :])
        acc = d if acc is None else acc + d
        k0 += kw
    o_ref[...] = acc.astype(o_ref.dtype)


def matmul(xs, w, l, res=None, out_dtype=F32, n_out=None, bm=1024, bn=1024):
    xs = list(xs) if isinstance(xs, (list, tuple)) else [xs]
    M = xs[0].shape[0]
    widths = tuple(x.shape[1] for x in xs)
    K, N = w.shape[1], n_out or w.shape[2]
    assert sum(widths) == K
    bm, bn = _blk(M, bm), _blk(N, bn)
    in_specs = [pl.BlockSpec((bm, kw), lambda i, j: (i, 0)) for kw in widths]
    in_specs.append(pl.BlockSpec((None, K, bn), lambda i, j: (l, 0, j)))
    args = xs + [w]
    if res is not None:
        in_specs.append(pl.BlockSpec((bm, bn), lambda i, j: (i, j)))
        args.append(res)
    return pl.pallas_call(
        functools.partial(_mm_body, widths=widths, has_res=res is not None),
        grid=(M // bm, N // bn), in_specs=in_specs,
        out_specs=pl.BlockSpec((bm, bn), lambda i, j: (i, j)),
        out_shape=jax.ShapeDtypeStruct((M, N), out_dtype),
        compiler_params=_cparams("parallel", "parallel"), name="matmul",
    )(*args)


def _ffn_act(a, u):
    return (0.5 * a * (1.0 + lax.erf(a * (2.0 ** -0.5))) * u).astype(BF16)


def _ffn_seq_body(h_ref, wg_ref, wu_ref, wd_ref, cw_ref, cb_ref, y_ref, o_ref, fst_ref, tail_ref, act_ref, *,
                  bm, blocks_per_seq, nf):
    i = pl.program_id(0)
    f = pl.program_id(1)

    @pl.when(f == 0)
    def _():
        act_ref[...] = jnp.zeros(act_ref.shape, act_ref.dtype)
        o_ref[...] = jnp.zeros(o_ref.shape, o_ref.dtype)

    @pl.when((i % blocks_per_seq == 0) & (f < nf))
    def _():
        tail_ref[f] = jnp.zeros(tail_ref.shape[1:], F32)

    @pl.when(f == nf)
    def _():
        o_ref[...] += _dot(act_ref[...], wd_ref[...])

    @pl.when(f < nf)
    def _():
        o_ref[...] += _dot(act_ref[...], wd_ref[...])
        h = h_ref[...]
        g = _dot(h, wg_ref[...])
        u = _dot(h, wu_ref[...])
        prev = tail_ref[f]
        p2, p1 = prev[0:1, :], prev[1:2, :]
        row = lax.broadcasted_iota(jnp.int32, g.shape, 0)
        s1 = jnp.where(row == 0, p1, pltpu.roll(g, 1, 0))
        s2 = jnp.where(row == 0, p2, jnp.where(row == 1, p1, pltpu.roll(g, 2, 0)))
        cw = cw_ref[...]
        a = s2 * cw[0:1, :] + s1 * cw[1:2, :] + g * cw[2:3, :] + cb_ref[...]
        last2 = g[bm - 2:bm, :]
        tail_ref[f, 0:2, :] = last2
        fst_ref[0, f] = last2
        act_ref[...] = _ffn_act(a, u)


def ffn_seq(h, wg, wu, wd, cw, cb, l, nseq, T, y, bm=1024, bf=256):
    M, D = nseq * T, h.shape[1]
    F = wg.shape[2]
    bm = _blk(T, bm)
    nf = F // bf
    bps = T // bm
    body = functools.partial(_ffn_seq_body, bm=bm, blocks_per_seq=bps, nf=nf)
    cur = lambda f: jnp.minimum(f, nf - 1)
    y, fst = pl.pallas_call(
        body, grid=(M // bm, nf + 1),
        in_specs=[pl.BlockSpec((bm, D), lambda i, f: (i, 0)),
                  pl.BlockSpec((None, D, bf), lambda i, f: (l, 0, cur(f))),
                  pl.BlockSpec((None, D, bf), lambda i, f: (l, 0, cur(f))),
                  pl.BlockSpec((None, bf, D), lambda i, f: (l, jnp.maximum(f - 1, 0), 0)),
                  pl.BlockSpec((None, FFN_CONV, bf), lambda i, f: (l, 0, cur(f))),
                  pl.BlockSpec((None, 1, bf), lambda i, f: (l, 0, cur(f))),
                  pl.BlockSpec(memory_space=pl.ANY)],
        out_specs=[pl.BlockSpec((bm, D), lambda i, f: (i, 0), pipeline_mode=pl.Buffered(1)),
                   pl.BlockSpec((1, nf, 2, bf), lambda i, f: (i // bps, 0, 0, 0))],
        out_shape=[jax.ShapeDtypeStruct((h.shape[0], D), F32), jax.ShapeDtypeStruct((nseq, nf, 2, bf), F32)],
        scratch_shapes=[pltpu.VMEM((nf, 8, bf), F32), pltpu.VMEM((bm, bf), BF16)],
        input_output_aliases={6: 0},
        compiler_params=_cparams("arbitrary", "arbitrary"), name="ffn_seq",
    )(h, wg, wu, wd, cw, cb.reshape(cb.shape[0], 1, F), y)
    return y, fst.transpose(0, 2, 1, 3).reshape(nseq, 2, F)


def _ffn_dec_body(h_ref, wg_ref, wu_ref, wd_ref, cw_ref, cb_ref, s0_ref, s1_ref, y_ref, o_ref, g_ref):
    f = pl.program_id(0)
    h = h_ref[...]
    g = _dot(h, wg_ref[...])
    u = _dot(h, wu_ref[...])
    g_ref[...] = g
    cw = cw_ref[...]
    a = s0_ref[...] * cw[0:1, :] + s1_ref[...] * cw[1:2, :] + g * cw[2:3, :] + cb_ref[...]
    d = _dot(_ffn_act(a, u), wd_ref[...])

    @pl.when(f == 0)
    def _():
        o_ref[...] = d

    @pl.when(f != 0)
    def _():
        o_ref[...] += d


def ffn_dec(h, wg, wu, wd, cw, cb, st, l, y, row0, bf=256):
    D = h.shape[1]
    B = st.shape[1]
    rb = row0 // B
    F = wg.shape[2]
    nf = F // bf
    st2 = st.reshape(st.shape[0], B, 2 * F)
    return pl.pallas_call(
        _ffn_dec_body, grid=(nf,),
        in_specs=[pl.BlockSpec((B, D), lambda f: (rb, 0)),
                  pl.BlockSpec((None, D, bf), lambda f: (l, 0, f)),
                  pl.BlockSpec((None, D, bf), lambda f: (l, 0, f)),
                  pl.BlockSpec((None, bf, D), lambda f: (l, f, 0)),
                  pl.BlockSpec((None, FFN_CONV, bf), lambda f: (l, 0, f)),
                  pl.BlockSpec((None, 1, bf), lambda f: (l, 0, f)),
                  pl.BlockSpec((None, B, bf), lambda f: (l, 0, f)),
                  pl.BlockSpec((None, B, bf), lambda f: (l, 0, nf + f)),
                  pl.BlockSpec(memory_space=pl.ANY)],
        out_specs=[pl.BlockSpec((B, D), lambda f: (rb, 0)), pl.BlockSpec((B, bf), lambda f: (0, f))],
        out_shape=[jax.ShapeDtypeStruct(y.shape, F32), jax.ShapeDtypeStruct((B, F), F32)],
        input_output_aliases={8: 0},
        compiler_params=_cparams("arbitrary"), name="ffn_dec",
    )(h, wg, wu, wd, cw, cb.reshape(cb.shape[0], 1, F), st2, st2, y)


def _log_gamma(h):
    return math.log(1.0 - 2.0 ** (-5.0 - h))


def _rope_tables(pos, half):
    inv = ROPE_THETA ** (-jnp.arange(half, dtype=F32) / half)
    ang = pos.astype(F32)[:, None] * inv[None, :]
    cos, sin = jnp.cos(ang), jnp.sin(ang)
    return jnp.concatenate([cos, cos], axis=-1), jnp.concatenate([-sin, sin], axis=-1)


def _rope(x, cosf, sins):
    return x * cosf + pltpu.roll(x, x.shape[-1] // 2, 1) * sins


def _ret_seq_body(q_ref, k_ref, v_ref, g_ref, cos_ref, sin_ref, o_ref, so_ref, s_ref, *, C, dk, dv):
    j = pl.program_id(1)

    @pl.when(j == 0)
    def _():
        s_ref[...] = jnp.zeros(s_ref.shape, F32)

    cosf, sins = cos_ref[...], sin_ref[...]
    ri = lax.broadcasted_iota(jnp.int32, (C, C), 0)
    ci = lax.broadcasted_iota(jnp.int32, (C, C), 1)
    idx = lax.broadcasted_iota(jnp.int32, (C, 1), 0).astype(F32)
    for h in range(RET_HEADS):
        lg = _log_gamma(h)
        q = _rope(q_ref[:, h * dk:(h + 1) * dk].astype(F32), cosf, sins)
        k = _rope(k_ref[:, h * dk:(h + 1) * dk].astype(F32), cosf, sins) * (dk ** -0.5)
        v = v_ref[:, h * dv:(h + 1) * dv]
        dmask = jnp.exp(jnp.where(ri >= ci, (ri - ci).astype(F32) * lg, NEG_BIG))
        scores = _dot_nt(q.astype(BF16), k.astype(BF16)) * dmask
        S = s_ref[h]
        qd = q * jnp.exp((idx + 1.0) * lg)
        o = _dot(scores.astype(BF16), v) + _dot(qd.astype(BF16), S.astype(BF16))
        kd = k * jnp.exp((C - 1.0 - idx) * lg)
        s_ref[h] = S * math.exp(C * lg) + _dot_tn(kd.astype(BF16), v)
        o = o * lax.rsqrt(jnp.mean(o * o, axis=-1, keepdims=True) + EPS)
        o_ref[:, h * dv:(h + 1) * dv] = (o * _silu(g_ref[:, h * dv:(h + 1) * dv].astype(F32))).astype(o_ref.dtype)

    @pl.when(j == pl.num_programs(1) - 1)
    def _():
        so_ref[0] = s_ref[...]


def retention_seq(z, B, T, dk, dv, C=256):
    C = _blk(T, C)
    nc = T // C
    qw, vw = RET_HEADS * dk, RET_HEADS * dv
    cosf, sins = _rope_tables(jnp.arange(T), dk // 2)
    body = functools.partial(_ret_seq_body, C=C, dk=dk, dv=dv)
    rows = lambda b, j: b * nc + j
    return pl.pallas_call(
        body, grid=(B, nc),
        in_specs=[pl.BlockSpec((C, qw), lambda b, j: (rows(b, j), 0)),
                  pl.BlockSpec((C, qw), lambda b, j: (rows(b, j), 1)),
                  pl.BlockSpec((C, vw), lambda b, j: (rows(b, j), (2 * qw) // vw)),
                  pl.BlockSpec((C, vw), lambda b, j: (rows(b, j), (2 * qw) // vw + 1)),
                  pl.BlockSpec((C, dk), lambda b, j: (j, 0)),
                  pl.BlockSpec((C, dk), lambda b, j: (j, 0))],
        out_specs=[pl.BlockSpec((C, vw), lambda b, j: (rows(b, j), 0)),
                   pl.BlockSpec((1, RET_HEADS, dk, dv), lambda b, j: (b, 0, 0, 0))],
        out_shape=[jax.ShapeDtypeStruct((B * T, vw), BF16), jax.ShapeDtypeStruct((B, RET_HEADS, dk, dv), F32)],
        scratch_shapes=[pltpu.VMEM((RET_HEADS, dk, dv), F32)],
        compiler_params=_cparams("arbitrary", "arbitrary"), name="retention_seq",
    )(z, z, z, z, cosf, sins)


def _row0(x, rows=8):
    r = lax.broadcasted_iota(jnp.int32, (rows, x.shape[1]), 0)
    return jnp.where(r == 0, jnp.broadcast_to(x, (rows, x.shape[1])), 0.0)


def _ret_dec_body(q_ref, k_ref, v_ref, g_ref, cos_ref, sin_ref, s_ref, acc_ref, o_ref, so_ref, qs_ref, *,
                  bb, dk, dv):
    cosf, sins = cos_ref[...], sin_ref[...]
    qs, ks = [], []
    for h in range(RET_HEADS):
        q = _rope(q_ref[:, h * dk:(h + 1) * dk].astype(F32), cosf, sins)
        k = _rope(k_ref[:, h * dk:(h + 1) * dk].astype(F32), cosf, sins) * (dk ** -0.5)
        qs.append(q)
        ks.append(k)
        for b in range(bb):
            r = _dot(_row0(q[b:b + 1]).astype(BF16), s_ref[b, h].astype(BF16))
            qs_ref[h, b:b + 1, :] = r[0:1]
    for h in range(RET_HEADS):
        gamma = math.exp(_log_gamma(h))
        v = v_ref[:, h * dv:(h + 1) * dv].astype(F32)
        o = gamma * qs_ref[h] + jnp.sum(qs[h] * ks[h], axis=-1, keepdims=True) * v
        o = o * lax.rsqrt(jnp.mean(o * o, axis=-1, keepdims=True) + EPS)
        gate = _silu(g_ref[:, h * dv:(h + 1) * dv].astype(F32))
        o_ref[:, h * dv:(h + 1) * dv] = (o * gate).astype(o_ref.dtype)
    for h in range(RET_HEADS):
        gamma = math.exp(_log_gamma(h))
        kT = ks[h].T
        v = v_ref[:, h * dv:(h + 1) * dv].astype(F32)
        for b in range(bb):
            kc = jnp.broadcast_to(kT[:, b:b + 1], (dk, dv))
            so_ref[b, h] = s_ref[b, h] * gamma + kc * v[b:b + 1]


def retention_dec(z, state, acc, l, dk, dv, bb=8):
    B = z.shape[0]
    bb = _blk(B, bb)
    qw, vw = RET_HEADS * dk, RET_HEADS * dv
    cosf, sins = _rope_tables(jnp.full((1,), PAST_LEN), dk // 2)
    body = functools.partial(_ret_dec_body, bb=bb, dk=dk, dv=dv)
    st = pl.BlockSpec((None, bb, RET_HEADS, dk, dv), lambda i: (l, i, 0, 0, 0))
    return pl.pallas_call(
        body, grid=(B // bb,),
        in_specs=[pl.BlockSpec((bb, qw), lambda i: (i, 0)),
                  pl.BlockSpec((bb, qw), lambda i: (i, 1)),
                  pl.BlockSpec((bb, vw), lambda i: (i, (2 * qw) // vw)),
                  pl.BlockSpec((bb, vw), lambda i: (i, (2 * qw) // vw + 1)),
                  pl.BlockSpec((1, dk), lambda i: (0, 0)),
                  pl.BlockSpec((1, dk), lambda i: (0, 0)), st, pl.BlockSpec(memory_space=pl.ANY)],
        out_specs=[pl.BlockSpec((bb, vw), lambda i: (i, 0)), st],
        out_shape=[jax.ShapeDtypeStruct((B, vw), BF16), jax.ShapeDtypeStruct(state.shape, F32)],
        input_output_aliases={7: 1},
        scratch_shapes=[pltpu.VMEM((RET_HEADS, bb, dv), F32)],
        compiler_params=_cparams("parallel"), name="retention_dec",
    )(z, z, z, z, cosf, sins, state, acc)


def _pool_seq_body(u_ref, w_ref, ls_ref, o_ref, ext_ref, *, R, gw):
    j = pl.program_id(1)

    @pl.when(j == 0)
    def _():
        ext_ref[0:16, :] = jnp.zeros((16, ext_ref.shape[1]), F32)

    ext_ref[16:16 + R, :] = u_ref[...].astype(F32)
    pos = j * R + lax.broadcasted_iota(jnp.int32, (R, 1), 0)
    for gi, win in enumerate(POOL_WINDOWS):
        sl = slice(gi * gw, (gi + 1) * gw)
        cur = ext_ref[16:16 + R, sl]
        acc = cur
        for s in range(1, win):
            acc = acc + ext_ref[16 - s:16 - s + R, sl]
        cnt = jnp.minimum(pos + 1, win).astype(F32)
        d = acc / cnt - cur
        o_ref[:, sl] = (_dot(d.astype(BF16), w_ref[gi]) * ls_ref[:, sl]).astype(o_ref.dtype)
    ext_ref[0:16, :] = ext_ref[R:R + 16, :]


def pool_seq(z, w_pool, ls, l, B, T, col_block, R=256):
    R = _blk(T, R)
    nr = T // R
    gw = w_pool.shape[-1]
    W = gw * len(POOL_WINDOWS)
    body = functools.partial(_pool_seq_body, R=R, gw=gw)
    return pl.pallas_call(
        body, grid=(B, nr),
        in_specs=[pl.BlockSpec((R, W), lambda b, j: (b * nr + j, col_block)),
                  pl.BlockSpec((None,) + w_pool.shape[1:], lambda b, j: (l, 0, 0, 0)),
                  pl.BlockSpec((None, 1, W), lambda b, j: (l, 0, 0))],
        out_specs=pl.BlockSpec((R, W), lambda b, j: (b * nr + j, 0)),
        out_shape=jax.ShapeDtypeStruct((B * T, W), BF16),
        scratch_shapes=[pltpu.VMEM((16 + R, W), F32)],
        compiler_params=_cparams("arbitrary", "arbitrary"), name="pool_seq",
    )(z, w_pool, ls.reshape(-1, 1, W))


def _pool_dec_body(u_ref, st_ref, w_ref, ls_ref, o_ref, *, gw, W):
    for gi, win in enumerate(POOL_WINDOWS):
        sl = slice(gi * gw, (gi + 1) * gw)
        cur = u_ref[:, sl].astype(F32)
        acc = cur
        for s in range(1, win):
            r = POOL_BUF - s
            acc = acc + st_ref[:, r * W + gi * gw:r * W + (gi + 1) * gw]
        cnt = float(min(PAST_LEN + 1, win))
        d = acc / cnt - cur
        o_ref[:, sl] = (_dot(d.astype(BF16), w_ref[gi]) * ls_ref[:, sl]).astype(o_ref.dtype)


def pool_dec(z, state, w_pool, ls, l, col_block, bb=64):
    B = z.shape[0]
    bb = _blk(B, bb)
    gw = w_pool.shape[-1]
    W = gw * len(POOL_WINDOWS)
    body = functools.partial(_pool_dec_body, gw=gw, W=W)
    return pl.pallas_call(
        body, grid=(B // bb,),
        in_specs=[pl.BlockSpec((bb, W), lambda i: (i, col_block)),
                  pl.BlockSpec((None, bb, POOL_BUF * W), lambda i: (l, i, 0)),
                  pl.BlockSpec((None,) + w_pool.shape[1:], lambda i: (l, 0, 0, 0)),
                  pl.BlockSpec((None, 1, W), lambda i: (l, 0, 0))],
        out_specs=pl.BlockSpec((bb, W), lambda i: (i, 0)),
        out_shape=jax.ShapeDtypeStruct((B, W), BF16),
        compiler_params=_cparams("parallel"), name="pool_dec",
    )(z, state.reshape(-1, B, POOL_BUF * W), w_pool, ls.reshape(-1, 1, W))


def _l2norm(x):
    return x * lax.rsqrt(jnp.sum(x * x, axis=-1, keepdims=True) + EPS)


def _softplus(x):
    return jnp.maximum(x, 0.0) + jnp.log1p(jnp.exp(-jnp.abs(x)))


def _sigmoid(x):
    return 1.0 / (1.0 + jnp.exp(-x))


def _solve_unit_lower(Ls, R, W):
    c = Ls[0].shape[0]
    n = len(Ls)
    half = GDN_SUB // 2
    blocks = []
    for a in range(c // GDN_SUB):
        r0 = a * GDN_SUB
        Rb = R[r0:r0 + GDN_SUB, :]
        if a:
            Xp = jnp.concatenate(blocks + [jnp.zeros((c - r0, n * W), F32)], axis=0).astype(BF16)
            Rb = Rb - jnp.concatenate(
                [_dot(Ls[h][r0:r0 + GDN_SUB, :].astype(BF16), Xp[:, h * W:(h + 1) * W]) for h in range(n)], axis=1)
        top, bot = Rb[:half], Rb[half:]
        for jj in range(GDN_SUB - 1):
            coef = jnp.concatenate(
                [jnp.broadcast_to(Ls[h][r0:r0 + GDN_SUB, r0 + jj:r0 + jj + 1], (GDN_SUB, W)) for h in range(n)], axis=1)
            row = top[jj:jj + 1] if jj < half else bot[jj - half:jj - half + 1]
            if jj < half - 1:
                top = top - coef[:half] * row
            bot = bot - coef[half:] * row
        blocks.append(jnp.concatenate([top, bot], axis=0))
    return jnp.concatenate(blocks, axis=0)


def _gdn_seq_body(q_ref, k_ref, v_ref, z_ref, wq_ref, wk_ref, wv_ref, ba_ref, al_ref, dt_ref, nw_ref,
                  o_ref, so_ref, s_ref, eq_ref, ek_ref, ev_ref, cq_ref, ck_ref, cv_ref,
                  l_s, at_s, rhs_s, qe_s, kd_s, uw_s, vn_s, op_s, *, R, dk, dv):
    r = pl.program_id(2)
    c = GDN_CHUNK
    hb = GDN_HB

    @pl.when(r == 0)
    def _():
        s_ref[...] = jnp.zeros(s_ref.shape, F32)
        for e in (eq_ref, ek_ref, ev_ref):
            e[...] = jnp.zeros(e.shape, F32)

    row8 = lax.broadcasted_iota(jnp.int32, eq_ref.shape, 0)
    for x_ref, w_ref, e_ref, c_ref in ((q_ref, wq_ref, eq_ref, cq_ref), (k_ref, wk_ref, ek_ref, ck_ref),
                                       (v_ref, wv_ref, ev_ref, cv_ref)):
        xf = x_ref[...].astype(F32)
        prev = e_ref[...]
        w = w_ref[...]
        acc = xf * w[GDN_CONV - 1:GDN_CONV, :]
        for s in range(1, GDN_CONV):
            xs = pltpu.roll(xf, s, 0)
            top = jnp.where(row8 < s, pltpu.roll(prev, s, 0), xs[0:8])
            xs = jnp.concatenate([top, xs[8:]], axis=0)
            acc = acc + xs * w[GDN_CONV - 1 - s:GDN_CONV - s, :]
        c_ref[...] = _silu(acc)
        e_ref[...] = xf[R - 8:R]

    ri = lax.broadcasted_iota(jnp.int32, (c, c), 0)
    ci = lax.broadcasted_iota(jnp.int32, (c, c), 1)
    tril = (ri >= ci).astype(F32)
    triu = (ri <= ci).astype(F32)
    neg_a = -jnp.exp(al_ref[0])
    dtb = dt_ref[0]
    nw = nw_ref[...]

    def chunk(ic, carry):
        r0 = pl.multiple_of(ic * c, c)
        rows = pl.ds(r0, c)
        x = ba_ref[rows, :]
        beta_all = _sigmoid(x)
        g_all = neg_a * _softplus(x + dtb)
        b_col = jnp.dot(tril, g_all, precision=lax.Precision.HIGHEST, preferred_element_type=F32)
        b_row = lax.dot_general(g_all, triu, (((0,), (0,)), ((), ())), precision=lax.Precision.HIGHEST,
                                preferred_element_type=F32)
        W = dv + dk
        for hh in range(hb):
            hs = slice(hh * dk, (hh + 1) * dk)
            q = _l2norm(cq_ref[rows, hs]) * (dk ** -0.5)
            k = _l2norm(ck_ref[rows, hs])
            v = cv_ref[rows, hh * dv:(hh + 1) * dv]
            bc = b_col[:, hb + hh:hb + hh + 1]
            br = b_row[hb + hh:hb + hh + 1, :]
            beta = beta_all[:, hh:hh + 1]
            decay = jnp.exp(jnp.where(ri >= ci, bc - br, NEG_BIG))
            kb = k * beta
            kbf = k.astype(BF16)
            l_s[hh] = jnp.where(ri > ci, _dot_nt(kb.astype(BF16), kbf) * decay, 0.0)
            at_s[hh] = (_dot_nt(q.astype(BF16), kbf) * decay).astype(BF16)
            eb = jnp.exp(bc)
            rhs_s[hh] = jnp.concatenate([v * beta, kb * eb], axis=1)
            qe_s[hh] = (q * eb).astype(BF16)
            kd_s[hh] = (k * jnp.exp(bc[c - 1:c, :] - bc)).astype(BF16)
        for g0 in range(0, hb, GDN_SOLVE):
            uw = _solve_unit_lower([l_s[h] for h in range(g0, g0 + GDN_SOLVE)],
                                   jnp.concatenate([rhs_s[h] for h in range(g0, g0 + GDN_SOLVE)], axis=1), W)
            for h in range(g0, g0 + GDN_SOLVE):
                uw_s[h] = uw[:, (h - g0) * W:(h - g0 + 1) * W]
        for hh in range(hb):
            Sb = s_ref[hh].astype(BF16)
            ws = _dot(jnp.concatenate([uw_s[hh, :, dv:].astype(BF16), qe_s[hh]], axis=0), Sb)
            vn_s[hh] = (uw_s[hh, :, :dv] - ws[:c]).astype(BF16)
            op_s[hh] = ws[c:]
        for hh in range(hb):
            vnb = vn_s[hh]
            o = op_s[hh] + _dot(at_s[hh], vnb)
            ebl = jnp.exp(b_col[c - 1:c, hb + hh:hb + hh + 1])
            s_ref[hh] = s_ref[hh] * ebl + _dot_tn(kd_s[hh], vnb)
            o = o * lax.rsqrt(jnp.mean(o * o, axis=-1, keepdims=True) + EPS) * nw
            gate = _silu(z_ref[rows, hh * dv:(hh + 1) * dv].astype(F32))
            o_ref[rows, hh * dv:(hh + 1) * dv] = (o * gate).astype(o_ref.dtype)
        return carry

    lax.fori_loop(0, R // c, chunk, 0)

    @pl.when(r == pl.num_programs(2) - 1)
    def _():
        so_ref[0] = s_ref[...]


def _group_lanes(t):
    hb, HG = GDN_HB, GDN_HEADS // GDN_HB
    t = t.reshape(t.shape[:-1] + (HG, hb))
    t = jnp.pad(t, [(0, 0)] * (t.ndim - 1) + [(hb, LANES - 2 * hb)])
    return t.reshape(t.shape[:-2] + (HG * LANES,))


def gdn_seq(z, zba, w_gconv, a_log, dt_bias, norm_w, l, B, T, dk, dv, col0, R=256):
    H, hb, c = GDN_HEADS, GDN_HB, GDN_CHUNK
    HG = H // hb
    R = _blk(T, R)
    nr = T // R
    M = B * T
    bw = hb * dk
    cb0 = col0 // bw
    per = (H * dk) // bw
    body = functools.partial(_gdn_seq_body, R=R, dk=dk, dv=dv)
    rows = lambda b, g, r: b * nr + r
    zspec = lambda off: pl.BlockSpec((R, bw), lambda b, g, r: (rows(b, g, r), cb0 + off + g))
    wspec = lambda off: pl.BlockSpec((None, GDN_CONV, bw), lambda b, g, r: (l, 0, off + g))
    pspec = pl.BlockSpec((1, 1, LANES), lambda b, g, r: (l * HG + g, 0, 0))
    return pl.pallas_call(
        body, grid=(B, HG, nr),
        in_specs=[zspec(0), zspec(per), zspec(2 * per), zspec(3 * per),
                  wspec(0), wspec(per), wspec(2 * per),
                  pl.BlockSpec((R, LANES), lambda b, g, r: (rows(b, g, r), g)), pspec, pspec,
                  pl.BlockSpec((None, 1, dv), lambda b, g, r: (l, 0, 0))],
        out_specs=[pl.BlockSpec((R, bw), lambda b, g, r: (rows(b, g, r), g)),
                   pl.BlockSpec((1, hb, dk, dv), lambda b, g, r: (b, g, 0, 0))],
        out_shape=[jax.ShapeDtypeStruct((M, H * dv), BF16), jax.ShapeDtypeStruct((B, H, dk, dv), F32)],
        scratch_shapes=[pltpu.VMEM((hb, dk, dv), F32)] + [pltpu.VMEM((8, bw), F32)] * 3
                       + [pltpu.VMEM((R, bw), F32)] * 3
                       + [pltpu.VMEM((hb, c, c), F32), pltpu.VMEM((hb, c, c), BF16),
                          pltpu.VMEM((hb, c, dv + dk), F32), pltpu.VMEM((hb, c, dk), BF16),
                          pltpu.VMEM((hb, c, dk), BF16), pltpu.VMEM((hb, c, dv + dk), F32),
                          pltpu.VMEM((hb, c, dv), BF16), pltpu.VMEM((hb, c, dv), F32)],
        compiler_params=_cparams("arbitrary", "arbitrary", "arbitrary"), name="gdn_seq",
    )(z, z, z, z, w_gconv, w_gconv, w_gconv, zba,
      _group_lanes(a_log).reshape(-1, 1, LANES), _group_lanes(dt_bias).reshape(-1, 1, LANES),
      norm_w.reshape(-1, 1, dv))


def _gdn_dec_body(xq_ref, xk_ref, xv_ref, z_ref, cs_ref, w_ref, b_ref, a_ref, al_ref, dt_ref, nw_ref, s_ref,
                  acc_ref, o_ref, so_ref, cu_ref, ks_ref, qs_ref, *, bb, dk, dv, CH):
    H = GDN_HEADS
    GW = CH // 3
    for gi, x_ref in enumerate((xq_ref, xk_ref, xv_ref)):
        cs = slice(gi * GW, (gi + 1) * GW)
        acc = x_ref[...].astype(F32) * w_ref[GDN_CONV - 1:GDN_CONV, cs]
        for i in range(GDN_CONV - 1):
            acc = acc + cs_ref[:, i * CH + gi * GW:i * CH + (gi + 1) * GW] * w_ref[i:i + 1, cs]
        cu_ref[:, cs] = _silu(acc)
    beta = _sigmoid(b_ref[...])
    eg = jnp.exp(-jnp.exp(al_ref[...]) * _softplus(a_ref[...] + dt_ref[...]))
    nw = nw_ref[...]
    qo, ko, vo = 0, H * dk, 2 * H * dk
    ks, qs = [], []
    for h in range(H):
        q = _l2norm(cu_ref[:, qo + h * dk:qo + (h + 1) * dk]) * (dk ** -0.5)
        k = _l2norm(cu_ref[:, ko + h * dk:ko + (h + 1) * dk])
        ks.append(k)
        qs.append(q)
        for b in range(bb):
            kq = jnp.concatenate([k[b:b + 1], q[b:b + 1], jnp.zeros((6, dk), F32)], axis=0)
            r = _dot(kq.astype(BF16), s_ref[b, h].astype(BF16))
            ks_ref[h, b:b + 1, :] = r[0:1]
            qs_ref[h, b:b + 1, :] = r[1:2]
    vns = []
    for h in range(H):
        q, k = qs[h], ks[h]
        v = cu_ref[:, vo + h * dv:vo + (h + 1) * dv]
        e = eg[:, h:h + 1]
        v_new = beta[:, h:h + 1] * (v - e * ks_ref[h])
        o = e * qs_ref[h] + jnp.sum(q * k, axis=-1, keepdims=True) * v_new
        o = o * lax.rsqrt(jnp.mean(o * o, axis=-1, keepdims=True) + EPS) * nw
        gate = _silu(z_ref[:, h * dv:(h + 1) * dv].astype(F32))
        o_ref[:, h * dv:(h + 1) * dv] = (o * gate).astype(o_ref.dtype)
        vns.append(v_new)
    for h in range(H):
        kT = ks[h].T
        for b in range(bb):
            kc = jnp.broadcast_to(kT[:, b:b + 1], (dk, dv))
            so_ref[b, h] = s_ref[b, h] * eg[b:b + 1, h:h + 1] + kc * vns[h][b:b + 1]


def gdn_dec(z, zb, za, conv_state, w_gconv, a_log, dt_bias, norm_w, state, acc, l, dk, dv, col0, bb=8):
    B = z.shape[0]
    H = GDN_HEADS
    bb = _blk(B, bb)
    CH = 3 * H * dk
    GW = H * dk
    cb0 = col0 // GW
    body = functools.partial(_gdn_dec_body, bb=bb, dk=dk, dv=dv, CH=CH)
    st = pl.BlockSpec((None, bb, H, dk, dv), lambda i: (l, i, 0, 0, 0))
    hrow = pl.BlockSpec((bb, H), lambda i: (i, 0))
    hpar = pl.BlockSpec((None, 1, H), lambda i: (l, 0, 0))
    zspec = lambda off: pl.BlockSpec((bb, GW), lambda i: (i, cb0 + off))
    return pl.pallas_call(
        body, grid=(B // bb,),
        in_specs=[zspec(0), zspec(1), zspec(2), zspec(3),
                  pl.BlockSpec((None, bb, (GDN_CONV - 1) * CH), lambda i: (l, i, 0)),
                  pl.BlockSpec((None, GDN_CONV, CH), lambda i: (l, 0, 0)),
                  hrow, hrow, hpar, hpar,
                  pl.BlockSpec((None, 1, dv), lambda i: (l, 0, 0)), st, pl.BlockSpec(memory_space=pl.ANY)],
        out_specs=[pl.BlockSpec((bb, H * dv), lambda i: (i, 0)), st],
        out_shape=[jax.ShapeDtypeStruct((B, H * dv), BF16), jax.ShapeDtypeStruct(state.shape, F32)],
        input_output_aliases={12: 1},
        scratch_shapes=[pltpu.VMEM((bb, CH), F32), pltpu.VMEM((H, bb, dv), F32), pltpu.VMEM((H, bb, dv), F32)],
        compiler_params=_cparams("parallel"), name="gdn_dec",
    )(z, z, z, z, conv_state.reshape(-1, B, (GDN_CONV - 1) * CH), w_gconv, zb, za,
      a_log.reshape(-1, 1, H), dt_bias.reshape(-1, 1, H), norm_w.reshape(-1, 1, dv), state, acc)


def _softmax_rows(s):
    m = jnp.max(s, axis=-1, keepdims=True)
    e = jnp.exp(s - m)
    return e / jnp.sum(e, axis=-1, keepdims=True)


def _xattn_seq_body(q_ref, k_ref, v_ref, o_ref, *, hd):
    for h in range(X_HEADS):
        sl = slice(h * hd, (h + 1) * hd)
        s = _dot_nt(q_ref[:, sl], k_ref[0, :, sl].astype(BF16)) * (hd ** -0.5)
        a = _softmax_rows(s)
        o_ref[:, sl] = _dot(a.astype(BF16), v_ref[0, :, sl].astype(BF16)).astype(o_ref.dtype)


def xattn_seq(q, mk, mv, B, T, bq=512):
    W = q.shape[1]
    hd = W // X_HEADS
    bq = _blk(T, bq)
    nq = T // bq
    mem = pl.BlockSpec((1,) + mk.shape[1:], lambda b, j: (b, 0, 0))
    return pl.pallas_call(
        functools.partial(_xattn_seq_body, hd=hd), grid=(B, nq),
        in_specs=[pl.BlockSpec((bq, W), lambda b, j: (b * nq + j, 0)), mem, mem],
        out_specs=pl.BlockSpec((bq, W), lambda b, j: (b * nq + j, 0)),
        out_shape=jax.ShapeDtypeStruct((B * T, W), BF16),
        compiler_params=_cparams("parallel", "parallel"), name="xattn_seq",
    )(q, mk, mv)


def _xattn_dec_body(q_ref, k_hbm, v_hbm, o_ref, kbuf, vbuf, sem, *, l, bb, hd, nsteps):
    i = pl.program_id(0)
    W = X_HEADS * hd

    def copies(step, slot):
        out = []
        for t, (src, dst) in enumerate(((k_hbm, kbuf), (v_hbm, vbuf))):
            for h in range(X_HEADS):
                out.append(pltpu.make_async_copy(src.at[l, pl.ds(step * bb, bb), :, h, :],
                                                 dst.at[slot, :, :, pl.ds(h * hd, hd)], sem.at[t, slot, h]))
        return out

    slot = i % 2

    @pl.when(i == 0)
    def _():
        for cp in copies(0, 0):
            cp.start()

    @pl.when(i + 1 < nsteps)
    def _():
        for cp in copies(i + 1, 1 - slot):
            cp.start()

    for cp in copies(i, slot):
        cp.wait()

    q = q_ref[...]
    row = lax.broadcasted_iota(jnp.int32, (8, W), 0)
    head = lax.broadcasted_iota(jnp.int32, (8, W), 1) // hd
    for b in range(bb):
        qd = jnp.where(row == head, jnp.broadcast_to(q[b:b + 1].astype(F32), (8, W)), 0.0).astype(BF16)
        s = _dot_nt(qd, kbuf[slot, b].astype(BF16)) * (hd ** -0.5)
        a = _softmax_rows(s)
        o = _dot(a.astype(BF16), vbuf[slot, b].astype(BF16))
        o_ref[b:b + 1, :] = jnp.sum(jnp.where(row == head, o, 0.0), axis=0, keepdims=True).astype(o_ref.dtype)


def xattn_dec(q, mk, mv, l, bb=8):
    B, W = q.shape
    hd = W // X_HEADS
    n_mem = mk.shape[2]
    bb = _blk(B, bb)
    nsteps = B // bb
    hbm = pl.BlockSpec(memory_space=pl.ANY)
    return pl.pallas_call(
        functools.partial(_xattn_dec_body, l=l, bb=bb, hd=hd, nsteps=nsteps), grid=(nsteps,),
        in_specs=[pl.BlockSpec((bb, W), lambda i: (i, 0)), hbm, hbm],
        out_specs=pl.BlockSpec((bb, W), lambda i: (i, 0)),
        out_shape=jax.ShapeDtypeStruct((B, W), BF16),
        scratch_shapes=[pltpu.VMEM((2, bb, n_mem, W), F32), pltpu.VMEM((2, bb, n_mem, W), F32),
                        pltpu.SemaphoreType.DMA((2, 2, X_HEADS))],
        compiler_params=_cparams("arbitrary"), name="xattn_dec",
    )(q, mk, mv)


def kernel(x_prompt, x_sample, state_ret, state_pool, state_gdn, state_gdn_conv, state_ffn_conv, cache_mem_k, cache_mem_v, mem_prompt, norm_mix, w_in, w_pool, ls_pool, w_gconv, gdn_a_log, gdn_dt_bias, gdn_norm, w_out, norm_x, norm_mem, w_xq, w_xk, w_xv, w_xo, norm_ffn, w_gate, w_up, w_fconv, b_fconv, w_down, norm_f):
    depth = w_in.shape[0]
    Bp, T, D = x_prompt.shape
    Bs = x_sample.shape[0]
    Mp = Bp * T
    assert x_sample.shape[1] == 1 and D % (32 * LANES) == 0 and Mp % Bs == 0
    n_main = w_in.shape[2] - 2 * GDN_HEADS
    hg = GDN_HEADS // GDN_HB
    w_b = w_in[:, :, n_main:n_main + GDN_HEADS].reshape(depth, D, hg, GDN_HB)
    w_a = w_in[:, :, n_main + GDN_HEADS:].reshape(depth, D, hg, GDN_HB)
    w_ba = jnp.pad(jnp.concatenate([w_b, w_a], axis=-1), ((0, 0), (0, 0), (0, 0), (0, LANES - 2 * GDN_HB)))
    w_ba = w_ba.reshape(depth, D, hg * LANES).astype(BF16)
    w_in16, w_out16, w_xq16, w_xo16 = (w.astype(BF16) for w in (w_in, w_out, w_xq, w_xo))
    w_xk16, w_xv16, w_pool16 = (w.astype(BF16) for w in (w_xk, w_xv, w_pool))
    ffn_par = (w_gate.astype(BF16), w_up.astype(BF16), w_down.astype(BF16), w_fconv, b_fconv)
    gdn_par = (w_gconv, gdn_a_log, gdn_dt_bias, gdn_norm)
    dk_r, dv_r = D // 32, D // 16
    dk_g = dv_g = D // 32
    pool_w = w_pool.shape[-1] * len(POOL_WINDOWS)
    off_pu = 2 * RET_HEADS * dk_r + 2 * RET_HEADS * dv_r
    off_c = off_pu + pool_w
    ch = 3 * GDN_HEADS * dk_g
    n_mem = mem_prompt.shape[1]
    memf = mem_prompt.reshape(Bp * n_mem, D)
    pk, pv = [], []
    for l in range(depth):
        mn = rmsnorm(memf, norm_mem, l)
        pk.append(matmul(mn, w_xk16, l).reshape(Bp, n_mem, -1))
        pv.append(matmul(mn, w_xv16, l).reshape(Bp, n_mem, -1))
    x = jnp.concatenate([x_prompt.reshape(Mp, D), x_sample.reshape(Bs, D)], axis=0)
    acc_ret, acc_gdn = lax.empty(state_ret.shape, F32), lax.empty(state_gdn.shape, F32)
    p_st = [[] for _ in range(5)]
    s_pool, s_gconv, s_fconv = [], [], []
    hn = rmsnorm(x, norm_mix, 0)
    for l in range(depth):
        z = matmul(hn, w_in16, l, out_dtype=BF16, n_out=n_main)
        zba = matmul(hn, w_ba, l, out_dtype=F32)
        ro_p, r_new = retention_seq(z, Bp, T, dk_r, dv_r)
        po_p = pool_seq(z, w_pool16, ls_pool, l, Bp, T, off_pu // pool_w)
        co_p, g_new = gdn_seq(z, zba, *gdn_par, l, Bp, T, dk_g, dv_g, off_c)
        z3 = z[:Mp].reshape(Bp, T, n_main)
        p_new = z3[:, T - POOL_BUF:, off_pu:off_pu + pool_w].astype(F32)
        gc_new = z3[:, T - (GDN_CONV - 1):, off_c:off_c + ch].astype(F32)
        zs = z[Mp:]
        zg = zba[Mp:].reshape(Bs, hg, LANES)
        zb = zg[:, :, :GDN_HB].reshape(Bs, GDN_HEADS)
        za = zg[:, :, GDN_HB:2 * GDN_HB].reshape(Bs, GDN_HEADS)
        ro_s, acc_ret = retention_dec(zs, state_ret, acc_ret, l, dk_r, dv_r)
        po_s = pool_dec(zs, state_pool, w_pool16, ls_pool, l, off_pu // pool_w)
        co_s, acc_gdn = gdn_dec(zs, zb, za, state_gdn_conv, *gdn_par, state_gdn, acc_gdn, l, dk_g, dv_g, off_c)
        s_pool.append(jnp.concatenate([state_pool[l, :, 1:], zs[:, None, off_pu:off_pu + pool_w].astype(F32)], axis=1))
        s_gconv.append(jnp.concatenate([state_gdn_conv[l, :, 1:], zs[:, None, off_c:off_c + ch].astype(F32)], axis=1))
        mix = [jnp.concatenate([a, b], axis=0) for a, b in ((ro_p, ro_s), (po_p, po_s), (co_p, co_s))]
        x = matmul(mix, w_out16, l, res=x)
        hx = rmsnorm(x, norm_x, l)
        q = matmul(hx, w_xq16, l, out_dtype=BF16)
        ao = jnp.concatenate([xattn_seq(q, pk[l], pv[l], Bp, T), xattn_dec(q[Mp:], cache_mem_k, cache_mem_v, l)], axis=0)
        x = matmul(ao, w_xo16, l, res=x)
        hf = rmsnorm(x, norm_ffn, l)
        y, f_new = ffn_seq(hf, *ffn_par, l, Bp, T, lax.empty(x.shape, F32))
        y, gate_row = ffn_dec(hf, *ffn_par, state_ffn_conv, l, y, Mp)
        s_fconv.append(jnp.concatenate([state_ffn_conv[l, :, 1:], gate_row[:, None, :]], axis=1))
        if l + 1 < depth:
            x, hn = add_rmsnorm(x, y, norm_mix, l + 1)
        else:
            x, hn = add_rmsnorm(x, y, norm_f.reshape(1, D), 0, out_dtype=F32)
        for lst, st in zip(p_st, (r_new, p_new, g_new, gc_new, f_new)):
            lst.append(st)
    p_mem_k = jnp.stack(pk).reshape(depth, Bp, n_mem, X_HEADS, -1)
    p_mem_v = jnp.stack(pv).reshape(depth, Bp, n_mem, X_HEADS, -1)
    return (hn[:Mp].reshape(Bp, T, D), hn[Mp:].reshape(Bs, 1, D), *(jnp.stack(st) for st in p_st), p_mem_k, p_mem_v,
            acc_ret, jnp.stack(s_pool), acc_gdn, jnp.stack(s_gconv), jnp.stack(s_fconv))
```

```python
import functools
import math

import jax
import jax.numpy as jnp
from jax import lax
from jax.experimental import pallas as pl
from jax.experimental.pallas import tpu as pltpu

F32 = jnp.float32
BF16 = jnp.bfloat16

EPS = 1e-6
ROPE_THETA = 10000.0
RET_HEADS = 4
GDN_HEADS = 16
X_HEADS = 4
POOL_WINDOWS = (2, 4, 8, 16)
POOL_BUF = max(POOL_WINDOWS) - 1
GDN_CONV = 4
FFN_CONV = 3
PAST_LEN = 16384

V7X_VMEM_BYTES = 64 * 1024 * 1024
VMEM_LIMIT = V7X_VMEM_BYTES - 8 * 1024 * 1024
LANES = 128
GDN_CHUNK = 64
GDN_SUB = 16
GDN_HB = 16
GDN_SOLVE = 8
NEG_BIG = -1e30


def _cparams(*sem):
    return pltpu.CompilerParams(dimension_semantics=sem, vmem_limit_bytes=VMEM_LIMIT)


def _dot(a, b):
    return jnp.dot(a, b, preferred_element_type=F32)


def _dot_nt(a, b):
    return lax.dot_general(a, b, (((1,), (1,)), ((), ())), preferred_element_type=F32)


def _dot_tn(a, b):
    return lax.dot_general(a, b, (((0,), (0,)), ((), ())), preferred_element_type=F32)


def _silu(x):
    return (0.5 * x) * (1.0 + jnp.tanh(0.5 * x))


def _blk(n, want):
    for step in (16, 8):
        for b in range(min(n, want) // step * step, 0, -step):
            if n % b == 0:
                return b
    return n


def _rmsnorm_body(x_ref, g_ref, o_ref):
    x = x_ref[...]
    y = x * lax.rsqrt(jnp.mean(x * x, axis=-1, keepdims=True) + EPS)
    o_ref[...] = (y * g_ref[...]).astype(o_ref.dtype)


def _gain_spec(l, D):
    return pl.BlockSpec((None, 1, D), lambda i: (l, 0, 0))


def rmsnorm(x, g, l, out_dtype=BF16):
    M, D = x.shape
    bm = _blk(M, 256)
    return pl.pallas_call(
        _rmsnorm_body, grid=(M // bm,),
        in_specs=[pl.BlockSpec((bm, D), lambda i: (i, 0)), _gain_spec(l, D)],
        out_specs=pl.BlockSpec((bm, D), lambda i: (i, 0)),
        out_shape=jax.ShapeDtypeStruct((M, D), out_dtype),
        compiler_params=_cparams("parallel"), name="rmsnorm",
    )(x, g.reshape(-1, 1, D))


def _add_rmsnorm_body(x_ref, y_ref, g_ref, s_ref, o_ref):
    x = x_ref[...] + y_ref[...]
    s_ref[...] = x
    y = x * lax.rsqrt(jnp.mean(x * x, axis=-1, keepdims=True) + EPS)
    o_ref[...] = (y * g_ref[...]).astype(o_ref.dtype)


def add_rmsnorm(x, y, g, l, out_dtype=BF16):
    M, D = x.shape
    bm = _blk(M, 256)
    row = pl.BlockSpec((bm, D), lambda i: (i, 0))
    return pl.pallas_call(
        _add_rmsnorm_body, grid=(M // bm,),
        in_specs=[row, row, _gain_spec(l, D)],
        out_specs=[row, row],
        out_shape=[jax.ShapeDtypeStruct((M, D), F32), jax.ShapeDtypeStruct((M, D), out_dtype)],
        compiler_params=_cparams("parallel"), name="add_rmsnorm",
    )(x, y, g.reshape(-1, 1, D))


def _mm_body(*refs, widths, has_res):
    xs, w_ref = refs[:len(widths)], refs[len(widths)]
    o_ref = refs[-1]
    acc = refs[len(widths) + 1][...] if has_res else None
    k0 = 0
    for x_ref, kw in zip(xs, widths):
        d = _dot(x_ref[...], w_ref[k0:k0 + kw, :])
        acc = d if acc is None else acc + d
        k0 += kw
    o_ref[...] = acc.astype(o_ref.dtype)


def matmul(xs, w, l, res=None, out_dtype=F32, n_out=None, bm=1024, bn=1024):
    xs = list(xs) if isinstance(xs, (list, tuple)) else [xs]
    M = xs[0].shape[0]
    widths = tuple(x.shape[1] for x in xs)
    K, N = w.shape[1], n_out or w.shape[2]
    assert sum(widths) == K
    bm, bn = _blk(M, bm), _blk(N, bn)
    in_specs = [pl.BlockSpec((bm, kw), lambda i, j: (i, 0)) for kw in widths]
    in_specs.append(pl.BlockSpec((None, K, bn), lambda i, j: (l, 0, j)))
    args = xs + [w]
    if res is not None:
        in_specs.append(pl.BlockSpec((bm, bn), lambda i, j: (i, j)))
        args.append(res)
    return pl.pallas_call(
        functools.partial(_mm_body, widths=widths, has_res=res is not None),
        grid=(M // bm, N // bn), in_specs=in_specs,
        out_specs=pl.BlockSpec((bm, bn), lambda i, j: (i, j)),
        out_shape=jax.ShapeDtypeStruct((M, N), out_dtype),
        compiler_params=_cparams("parallel", "parallel"), name="matmul",
    )(*args)


def _ffn_act(a, u):
    return (0.5 * a * (1.0 + lax.erf(a * (2.0 ** -0.5))) * u).astype(BF16)


def _ffn_seq_body(h_ref, wg_ref, wu_ref, wd_ref, cw_ref, cb_ref, o_ref, fst_ref, tail_ref, act_ref, *,
                  bm, blocks_per_seq, nf):
    i = pl.program_id(0)
    f = pl.program_id(1)

    @pl.when(f == 0)
    def _():
        act_ref[...] = jnp.zeros(act_ref.shape, act_ref.dtype)
        o_ref[...] = jnp.zeros(o_ref.shape, o_ref.dtype)

    @pl.when((i % blocks_per_seq == 0) & (f < nf))
    def _():
        tail_ref[f] = jnp.zeros(tail_ref.shape[1:], F32)

    @pl.when(f == nf)
    def _():
        o_ref[...] += _dot(act_ref[...], wd_ref[...])

    @pl.when(f < nf)
    def _():
        o_ref[...] += _dot(act_ref[...], wd_ref[...])
        h = h_ref[...]
        g = _dot(h, wg_ref[...])
        u = _dot(h, wu_ref[...])
        prev = tail_ref[f]
        p2, p1 = prev[0:1, :], prev[1:2, :]
        row = lax.broadcasted_iota(jnp.int32, g.shape, 0)
        s1 = jnp.where(row == 0, p1, pltpu.roll(g, 1, 0))
        s2 = jnp.where(row == 0, p2, jnp.where(row == 1, p1, pltpu.roll(g, 2, 0)))
        cw = cw_ref[...]
        a = s2 * cw[0:1, :] + s1 * cw[1:2, :] + g * cw[2:3, :] + cb_ref[...]
        last2 = g[bm - 2:bm, :]
        tail_ref[f, 0:2, :] = last2
        fst_ref[0, f] = last2
        act_ref[...] = _ffn_act(a, u)


def ffn_seq(h, wg, wu, wd, cw, cb, l, nseq, bm=1024, bf=256):
    M, D = h.shape
    F = wg.shape[2]
    T = M // nseq
    bm = _blk(T, bm)
    nf = F // bf
    bps = T // bm
    body = functools.partial(_ffn_seq_body, bm=bm, blocks_per_seq=bps, nf=nf)
    cur = lambda f: jnp.minimum(f, nf - 1)
    y, fst = pl.pallas_call(
        body, grid=(M // bm, nf + 1),
        in_specs=[pl.BlockSpec((bm, D), lambda i, f: (i, 0)),
                  pl.BlockSpec((None, D, bf), lambda i, f: (l, 0, cur(f))),
                  pl.BlockSpec((None, D, bf), lambda i, f: (l, 0, cur(f))),
                  pl.BlockSpec((None, bf, D), lambda i, f: (l, jnp.maximum(f - 1, 0), 0)),
                  pl.BlockSpec((None, FFN_CONV, bf), lambda i, f: (l, 0, cur(f))),
                  pl.BlockSpec((None, 1, bf), lambda i, f: (l, 0, cur(f)))],
        out_specs=[pl.BlockSpec((bm, D), lambda i, f: (i, 0), pipeline_mode=pl.Buffered(1)),
                   pl.BlockSpec((1, nf, 2, bf), lambda i, f: (i // bps, 0, 0, 0))],
        out_shape=[jax.ShapeDtypeStruct((M, D), F32), jax.ShapeDtypeStruct((nseq, nf, 2, bf), F32)],
        scratch_shapes=[pltpu.VMEM((nf, 8, bf), F32), pltpu.VMEM((bm, bf), BF16)],
        compiler_params=_cparams("arbitrary", "arbitrary"), name="ffn_seq",
    )(h, wg, wu, wd, cw, cb.reshape(cb.shape[0], 1, F))
    return y, fst.transpose(0, 2, 1, 3).reshape(nseq, 2, F)


def _ffn_dec_body(h_ref, wg_ref, wu_ref, wd_ref, cw_ref, cb_ref, s0_ref, s1_ref, o_ref, g_ref):
    f = pl.program_id(0)
    h = h_ref[...]
    g = _dot(h, wg_ref[...])
    u = _dot(h, wu_ref[...])
    g_ref[...] = g
    cw = cw_ref[...]
    a = s0_ref[...] * cw[0:1, :] + s1_ref[...] * cw[1:2, :] + g * cw[2:3, :] + cb_ref[...]
    d = _dot(_ffn_act(a, u), wd_ref[...])

    @pl.when(f == 0)
    def _():
        o_ref[...] = d

    @pl.when(f != 0)
    def _():
        o_ref[...] += d


def ffn_dec(h, wg, wu, wd, cw, cb, st, l, bf=256):
    B, D = h.shape
    F = wg.shape[2]
    nf = F // bf
    st2 = st.reshape(st.shape[0], B, 2 * F)
    return pl.pallas_call(
        _ffn_dec_body, grid=(nf,),
        in_specs=[pl.BlockSpec((B, D), lambda f: (0, 0)),
                  pl.BlockSpec((None, D, bf), lambda f: (l, 0, f)),
                  pl.BlockSpec((None, D, bf), lambda f: (l, 0, f)),
                  pl.BlockSpec((None, bf, D), lambda f: (l, f, 0)),
                  pl.BlockSpec((None, FFN_CONV, bf), lambda f: (l, 0, f)),
                  pl.BlockSpec((None, 1, bf), lambda f: (l, 0, f)),
                  pl.BlockSpec((None, B, bf), lambda f: (l, 0, f)),
                  pl.BlockSpec((None, B, bf), lambda f: (l, 0, nf + f))],
        out_specs=[pl.BlockSpec((B, D), lambda f: (0, 0)), pl.BlockSpec((B, bf), lambda f: (0, f))],
        out_shape=[jax.ShapeDtypeStruct((B, D), F32), jax.ShapeDtypeStruct((B, F), F32)],
        compiler_params=_cparams("arbitrary"), name="ffn_dec",
    )(h, wg, wu, wd, cw, cb.reshape(cb.shape[0], 1, F), st2, st2)


def _log_gamma(h):
    return math.log(1.0 - 2.0 ** (-5.0 - h))


def _rope_tables(pos, half):
    inv = ROPE_THETA ** (-jnp.arange(half, dtype=F32) / half)
    ang = pos.astype(F32)[:, None] * inv[None, :]
    cos, sin = jnp.cos(ang), jnp.sin(ang)
    return jnp.concatenate([cos, cos], axis=-1), jnp.concatenate([-sin, sin], axis=-1)


def _rope(x, cosf, sins):
    return x * cosf + pltpu.roll(x, x.shape[-1] // 2, 1) * sins


def _ret_seq_body(q_ref, k_ref, v_ref, g_ref, cos_ref, sin_ref, o_ref, so_ref, s_ref, *, C, dk, dv):
    j = pl.program_id(1)

    @pl.when(j == 0)
    def _():
        s_ref[...] = jnp.zeros(s_ref.shape, F32)

    cosf, sins = cos_ref[...], sin_ref[...]
    ri = lax.broadcasted_iota(jnp.int32, (C, C), 0)
    ci = lax.broadcasted_iota(jnp.int32, (C, C), 1)
    idx = lax.broadcasted_iota(jnp.int32, (C, 1), 0).astype(F32)
    for h in range(RET_HEADS):
        lg = _log_gamma(h)
        q = _rope(q_ref[:, h * dk:(h + 1) * dk].astype(F32), cosf, sins)
        k = _rope(k_ref[:, h * dk:(h + 1) * dk].astype(F32), cosf, sins) * (dk ** -0.5)
        v = v_ref[:, h * dv:(h + 1) * dv]
        dmask = jnp.exp(jnp.where(ri >= ci, (ri - ci).astype(F32) * lg, NEG_BIG))
        scores = _dot_nt(q.astype(BF16), k.astype(BF16)) * dmask
        S = s_ref[h]
        qd = q * jnp.exp((idx + 1.0) * lg)
        o = _dot(scores.astype(BF16), v) + _dot(qd.astype(BF16), S.astype(BF16))
        kd = k * jnp.exp((C - 1.0 - idx) * lg)
        s_ref[h] = S * math.exp(C * lg) + _dot_tn(kd.astype(BF16), v)
        o = o * lax.rsqrt(jnp.mean(o * o, axis=-1, keepdims=True) + EPS)
        o_ref[:, h * dv:(h + 1) * dv] = (o * _silu(g_ref[:, h * dv:(h + 1) * dv].astype(F32))).astype(o_ref.dtype)

    @pl.when(j == pl.num_programs(1) - 1)
    def _():
        so_ref[0] = s_ref[...]


def retention_seq(z, B, T, dk, dv, C=256):
    C = _blk(T, C)
    nc = T // C
    qw, vw = RET_HEADS * dk, RET_HEADS * dv
    cosf, sins = _rope_tables(jnp.arange(T), dk // 2)
    body = functools.partial(_ret_seq_body, C=C, dk=dk, dv=dv)
    rows = lambda b, j: b * nc + j
    return pl.pallas_call(
        body, grid=(B, nc),
        in_specs=[pl.BlockSpec((C, qw), lambda b, j: (rows(b, j), 0)),
                  pl.BlockSpec((C, qw), lambda b, j: (rows(b, j), 1)),
                  pl.BlockSpec((C, vw), lambda b, j: (rows(b, j), (2 * qw) // vw)),
                  pl.BlockSpec((C, vw), lambda b, j: (rows(b, j), (2 * qw) // vw + 1)),
                  pl.BlockSpec((C, dk), lambda b, j: (j, 0)),
                  pl.BlockSpec((C, dk), lambda b, j: (j, 0))],
        out_specs=[pl.BlockSpec((C, vw), lambda b, j: (rows(b, j), 0)),
                   pl.BlockSpec((1, RET_HEADS, dk, dv), lambda b, j: (b, 0, 0, 0))],
        out_shape=[jax.ShapeDtypeStruct((B * T, vw), BF16), jax.ShapeDtypeStruct((B, RET_HEADS, dk, dv), F32)],
        scratch_shapes=[pltpu.VMEM((RET_HEADS, dk, dv), F32)],
        compiler_params=_cparams("arbitrary", "arbitrary"), name="retention_seq",
    )(z, z, z, z, cosf, sins)


def _row0(x, rows=8):
    r = lax.broadcasted_iota(jnp.int32, (rows, x.shape[1]), 0)
    return jnp.where(r == 0, jnp.broadcast_to(x, (rows, x.shape[1])), 0.0)


def _ret_dec_body(q_ref, k_ref, v_ref, g_ref, cos_ref, sin_ref, s_ref, acc_ref, o_ref, so_ref, qs_ref, *,
                  bb, dk, dv):
    cosf, sins = cos_ref[...], sin_ref[...]
    qs, ks = [], []
    for h in range(RET_HEADS):
        q = _rope(q_ref[:, h * dk:(h + 1) * dk].astype(F32), cosf, sins)
        k = _rope(k_ref[:, h * dk:(h + 1) * dk].astype(F32), cosf, sins) * (dk ** -0.5)
        qs.append(q)
        ks.append(k)
        for b in range(bb):
            r = _dot(_row0(q[b:b + 1]).astype(BF16), s_ref[b, h].astype(BF16))
            qs_ref[h, b:b + 1, :] = r[0:1]
    for h in range(RET_HEADS):
        gamma = math.exp(_log_gamma(h))
        v = v_ref[:, h * dv:(h + 1) * dv].astype(F32)
        o = gamma * qs_ref[h] + jnp.sum(qs[h] * ks[h], axis=-1, keepdims=True) * v
        o = o * lax.rsqrt(jnp.mean(o * o, axis=-1, keepdims=True) + EPS)
        gate = _silu(g_ref[:, h * dv:(h + 1) * dv].astype(F32))
        o_ref[:, h * dv:(h + 1) * dv] = (o * gate).astype(o_ref.dtype)
    for h in range(RET_HEADS):
        gamma = math.exp(_log_gamma(h))
        kT = ks[h].T
        v = v_ref[:, h * dv:(h + 1) * dv].astype(F32)
        for b in range(bb):
            kc = jnp.broadcast_to(kT[:, b:b + 1], (dk, dv))
            so_ref[b, h] = s_ref[b, h] * gamma + kc * v[b:b + 1]


def retention_dec(z, state, acc, l, dk, dv, bb=8):
    B = z.shape[0]
    bb = _blk(B, bb)
    qw, vw = RET_HEADS * dk, RET_HEADS * dv
    cosf, sins = _rope_tables(jnp.full((1,), PAST_LEN), dk // 2)
    body = functools.partial(_ret_dec_body, bb=bb, dk=dk, dv=dv)
    st = pl.BlockSpec((None, bb, RET_HEADS, dk, dv), lambda i: (l, i, 0, 0, 0))
    return pl.pallas_call(
        body, grid=(B // bb,),
        in_specs=[pl.BlockSpec((bb, qw), lambda i: (i, 0)),
                  pl.BlockSpec((bb, qw), lambda i: (i, 1)),
                  pl.BlockSpec((bb, vw), lambda i: (i, (2 * qw) // vw)),
                  pl.BlockSpec((bb, vw), lambda i: (i, (2 * qw) // vw + 1)),
                  pl.BlockSpec((1, dk), lambda i: (0, 0)),
                  pl.BlockSpec((1, dk), lambda i: (0, 0)), st, pl.BlockSpec(memory_space=pl.ANY)],
        out_specs=[pl.BlockSpec((bb, vw), lambda i: (i, 0)), st],
        out_shape=[jax.ShapeDtypeStruct((B, vw), BF16), jax.ShapeDtypeStruct(state.shape, F32)],
        input_output_aliases={7: 1},
        scratch_shapes=[pltpu.VMEM((RET_HEADS, bb, dv), F32)],
        compiler_params=_cparams("parallel"), name="retention_dec",
    )(z, z, z, z, cosf, sins, state, acc)


def _pool_seq_body(u_ref, w_ref, ls_ref, o_ref, ext_ref, *, R, gw):
    j = pl.program_id(1)

    @pl.when(j == 0)
    def _():
        ext_ref[0:16, :] = jnp.zeros((16, ext_ref.shape[1]), F32)

    ext_ref[16:16 + R, :] = u_ref[...].astype(F32)
    pos = j * R + lax.broadcasted_iota(jnp.int32, (R, 1), 0)
    for gi, win in enumerate(POOL_WINDOWS):
        sl = slice(gi * gw, (gi + 1) * gw)
        cur = ext_ref[16:16 + R, sl]
        acc = cur
        for s in range(1, win):
            acc = acc + ext_ref[16 - s:16 - s + R, sl]
        cnt = jnp.minimum(pos + 1, win).astype(F32)
        d = acc / cnt - cur
        o_ref[:, sl] = (_dot(d.astype(BF16), w_ref[gi]) * ls_ref[:, sl]).astype(o_ref.dtype)
    ext_ref[0:16, :] = ext_ref[R:R + 16, :]


def pool_seq(z, w_pool, ls, l, B, T, col_block, R=256):
    R = _blk(T, R)
    nr = T // R
    gw = w_pool.shape[-1]
    W = gw * len(POOL_WINDOWS)
    body = functools.partial(_pool_seq_body, R=R, gw=gw)
    return pl.pallas_call(
        body, grid=(B, nr),
        in_specs=[pl.BlockSpec((R, W), lambda b, j: (b * nr + j, col_block)),
                  pl.BlockSpec((None,) + w_pool.shape[1:], lambda b, j: (l, 0, 0, 0)),
                  pl.BlockSpec((None, 1, W), lambda b, j: (l, 0, 0))],
        out_specs=pl.BlockSpec((R, W), lambda b, j: (b * nr + j, 0)),
        out_shape=jax.ShapeDtypeStruct((B * T, W), BF16),
        scratch_shapes=[pltpu.VMEM((16 + R, W), F32)],
        compiler_params=_cparams("arbitrary", "arbitrary"), name="pool_seq",
    )(z, w_pool, ls.reshape(-1, 1, W))


def _pool_dec_body(u_ref, st_ref, w_ref, ls_ref, o_ref, *, gw, W):
    for gi, win in enumerate(POOL_WINDOWS):
        sl = slice(gi * gw, (gi + 1) * gw)
        cur = u_ref[:, sl].astype(F32)
        acc = cur
        for s in range(1, win):
            r = POOL_BUF - s
            acc = acc + st_ref[:, r * W + gi * gw:r * W + (gi + 1) * gw]
        cnt = float(min(PAST_LEN + 1, win))
        d = acc / cnt - cur
        o_ref[:, sl] = (_dot(d.astype(BF16), w_ref[gi]) * ls_ref[:, sl]).astype(o_ref.dtype)


def pool_dec(z, state, w_pool, ls, l, col_block, bb=64):
    B = z.shape[0]
    bb = _blk(B, bb)
    gw = w_pool.shape[-1]
    W = gw * len(POOL_WINDOWS)
    body = functools.partial(_pool_dec_body, gw=gw, W=W)
    return pl.pallas_call(
        body, grid=(B // bb,),
        in_specs=[pl.BlockSpec((bb, W), lambda i: (i, col_block)),
                  pl.BlockSpec((None, bb, POOL_BUF * W), lambda i: (l, i, 0)),
                  pl.BlockSpec((None,) + w_pool.shape[1:], lambda i: (l, 0, 0, 0)),
                  pl.BlockSpec((None, 1, W), lambda i: (l, 0, 0))],
        out_specs=pl.BlockSpec((bb, W), lambda i: (i, 0)),
        out_shape=jax.ShapeDtypeStruct((B, W), BF16),
        compiler_params=_cparams("parallel"), name="pool_dec",
    )(z, state.reshape(-1, B, POOL_BUF * W), w_pool, ls.reshape(-1, 1, W))


def _l2norm(x):
    return x * lax.rsqrt(jnp.sum(x * x, axis=-1, keepdims=True) + EPS)


def _softplus(x):
    return jnp.maximum(x, 0.0) + jnp.log1p(jnp.exp(-jnp.abs(x)))


def _sigmoid(x):
    return 1.0 / (1.0 + jnp.exp(-x))


def _solve_unit_lower(Ls, R, W):
    c = Ls[0].shape[0]
    n = len(Ls)
    half = GDN_SUB // 2
    blocks = []
    for a in range(c // GDN_SUB):
        r0 = a * GDN_SUB
        Rb = R[r0:r0 + GDN_SUB, :]
        if a:
            Xp = jnp.concatenate(blocks + [jnp.zeros((c - r0, n * W), F32)], axis=0).astype(BF16)
            Rb = Rb - jnp.concatenate(
                [_dot(Ls[h][r0:r0 + GDN_SUB, :].astype(BF16), Xp[:, h * W:(h + 1) * W]) for h in range(n)], axis=1)
        top, bot = Rb[:half], Rb[half:]
        for jj in range(GDN_SUB - 1):
            coef = jnp.concatenate(
                [jnp.broadcast_to(Ls[h][r0:r0 + GDN_SUB, r0 + jj:r0 + jj + 1], (GDN_SUB, W)) for h in range(n)], axis=1)
            row = top[jj:jj + 1] if jj < half else bot[jj - half:jj - half + 1]
            if jj < half - 1:
                top = top - coef[:half] * row
            bot = bot - coef[half:] * row
        blocks.append(jnp.concatenate([top, bot], axis=0))
    return jnp.concatenate(blocks, axis=0)


def _gdn_seq_body(q_ref, k_ref, v_ref, z_ref, wq_ref, wk_ref, wv_ref, ba_ref, al_ref, dt_ref, nw_ref,
                  o_ref, so_ref, s_ref, eq_ref, ek_ref, ev_ref, cq_ref, ck_ref, cv_ref,
                  l_s, at_s, rhs_s, qe_s, kd_s, uw_s, vn_s, op_s, *, R, dk, dv):
    r = pl.program_id(2)
    c = GDN_CHUNK
    hb = GDN_HB

    @pl.when(r == 0)
    def _():
        s_ref[...] = jnp.zeros(s_ref.shape, F32)
        for e in (eq_ref, ek_ref, ev_ref):
            e[...] = jnp.zeros(e.shape, F32)

    row8 = lax.broadcasted_iota(jnp.int32, eq_ref.shape, 0)
    for x_ref, w_ref, e_ref, c_ref in ((q_ref, wq_ref, eq_ref, cq_ref), (k_ref, wk_ref, ek_ref, ck_ref),
                                       (v_ref, wv_ref, ev_ref, cv_ref)):
        xf = x_ref[...].astype(F32)
        prev = e_ref[...]
        w = w_ref[...]
        acc = xf * w[GDN_CONV - 1:GDN_CONV, :]
        for s in range(1, GDN_CONV):
            xs = pltpu.roll(xf, s, 0)
            top = jnp.where(row8 < s, pltpu.roll(prev, s, 0), xs[0:8])
            xs = jnp.concatenate([top, xs[8:]], axis=0)
            acc = acc + xs * w[GDN_CONV - 1 - s:GDN_CONV - s, :]
        c_ref[...] = _silu(acc)
        e_ref[...] = xf[R - 8:R]

    ri = lax.broadcasted_iota(jnp.int32, (c, c), 0)
    ci = lax.broadcasted_iota(jnp.int32, (c, c), 1)
    tril = (ri >= ci).astype(F32)
    triu = (ri <= ci).astype(F32)
    neg_a = -jnp.exp(al_ref[0])
    dtb = dt_ref[0]
    nw = nw_ref[...]

    def chunk(ic, carry):
        r0 = pl.multiple_of(ic * c, c)
        rows = pl.ds(r0, c)
        x = ba_ref[rows, :]
        beta_all = _sigmoid(x)
        g_all = neg_a * _softplus(x + dtb)
        b_col = jnp.dot(tril, g_all, precision=lax.Precision.HIGHEST, preferred_element_type=F32)
        b_row = lax.dot_general(g_all, triu, (((0,), (0,)), ((), ())), precision=lax.Precision.HIGHEST,
                                preferred_element_type=F32)
        W = dv + dk
        for hh in range(hb):
            hs = slice(hh * dk, (hh + 1) * dk)
            q = _l2norm(cq_ref[rows, hs]) * (dk ** -0.5)
            k = _l2norm(ck_ref[rows, hs])
            v = cv_ref[rows, hh * dv:(hh + 1) * dv]
            bc = jnp.broadcast_to(b_col[:, hb + hh:hb + hh + 1], (c, dk))
            br = b_row[hb + hh:hb + hh + 1, :]
            beta = jnp.broadcast_to(beta_all[:, hh:hh + 1], (c, dk))
            decay = jnp.exp(jnp.where(ri >= ci, bc[:, :c] - br, NEG_BIG))
            kb = k * beta
            kbf = k.astype(BF16)
            l_s[hh] = jnp.where(ri > ci, _dot_nt(kb.astype(BF16), kbf) * decay, 0.0)
            at_s[hh] = (_dot_nt(q.astype(BF16), kbf) * decay).astype(BF16)
            eb = jnp.exp(bc)
            rhs_s[hh] = jnp.concatenate([v * beta, kb * eb], axis=1)
            qe_s[hh] = (q * eb).astype(BF16)
            kd_s[hh] = (k * jnp.exp(bc[c - 1:c, :] - bc)).astype(BF16)
        for g0 in range(0, hb, GDN_SOLVE):
            uw = _solve_unit_lower([l_s[h] for h in range(g0, g0 + GDN_SOLVE)],
                                   jnp.concatenate([rhs_s[h] for h in range(g0, g0 + GDN_SOLVE)], axis=1), W)
            for h in range(g0, g0 + GDN_SOLVE):
                uw_s[h] = uw[:, (h - g0) * W:(h - g0 + 1) * W]
        for hh in range(hb):
            Sb = s_ref[hh].astype(BF16)
            ws = _dot(jnp.concatenate([uw_s[hh, :, dv:].astype(BF16), qe_s[hh]], axis=0), Sb)
            vn_s[hh] = (uw_s[hh, :, :dv] - ws[:c]).astype(BF16)
            op_s[hh] = ws[c:]
        for hh in range(hb):
            vnb = vn_s[hh]
            o = op_s[hh] + _dot(at_s[hh], vnb)
            ebl = jnp.exp(b_col[c - 1:c, hb + hh:hb + hh + 1])
            s_ref[hh] = s_ref[hh] * ebl + _dot_tn(kd_s[hh], vnb)
            o = o * lax.rsqrt(jnp.mean(o * o, axis=-1, keepdims=True) + EPS) * nw
            gate = _silu(z_ref[rows, hh * dv:(hh + 1) * dv].astype(F32))
            o_ref[rows, hh * dv:(hh + 1) * dv] = (o * gate).astype(o_ref.dtype)
        return carry

    lax.fori_loop(0, R // c, chunk, 0)

    @pl.when(r == pl.num_programs(2) - 1)
    def _():
        so_ref[0] = s_ref[...]


def _group_lanes(t):
    hb, HG = GDN_HB, GDN_HEADS // GDN_HB
    t = t.reshape(t.shape[:-1] + (HG, hb))
    t = jnp.pad(t, [(0, 0)] * (t.ndim - 1) + [(hb, LANES - 2 * hb)])
    return t.reshape(t.shape[:-2] + (HG * LANES,))


def gdn_seq(z, zba, w_gconv, a_log, dt_bias, norm_w, l, B, T, dk, dv, col0, R=256):
    H, hb, c = GDN_HEADS, GDN_HB, GDN_CHUNK
    HG = H // hb
    R = _blk(T, R)
    nr = T // R
    M = B * T
    bw = hb * dk
    cb0 = col0 // bw
    per = (H * dk) // bw
    body = functools.partial(_gdn_seq_body, R=R, dk=dk, dv=dv)
    rows = lambda b, g, r: b * nr + r
    zspec = lambda off: pl.BlockSpec((R, bw), lambda b, g, r: (rows(b, g, r), cb0 + off + g))
    wspec = lambda off: pl.BlockSpec((None, GDN_CONV, bw), lambda b, g, r: (l, 0, off + g))
    pspec = pl.BlockSpec((1, 1, LANES), lambda b, g, r: (l * HG + g, 0, 0))
    return pl.pallas_call(
        body, grid=(B, HG, nr),
        in_specs=[zspec(0), zspec(per), zspec(2 * per), zspec(3 * per),
                  wspec(0), wspec(per), wspec(2 * per),
                  pl.BlockSpec((R, LANES), lambda b, g, r: (rows(b, g, r), g)), pspec, pspec,
                  pl.BlockSpec((None, 1, dv), lambda b, g, r: (l, 0, 0))],
        out_specs=[pl.BlockSpec((R, bw), lambda b, g, r: (rows(b, g, r), g)),
                   pl.BlockSpec((1, hb, dk, dv), lambda b, g, r: (b, g, 0, 0))],
        out_shape=[jax.ShapeDtypeStruct((M, H * dv), BF16), jax.ShapeDtypeStruct((B, H, dk, dv), F32)],
        scratch_shapes=[pltpu.VMEM((hb, dk, dv), F32)] + [pltpu.VMEM((8, bw), F32)] * 3
                       + [pltpu.VMEM((R, bw), F32)] * 3
                       + [pltpu.VMEM((hb, c, c), F32), pltpu.VMEM((hb, c, c), BF16),
                          pltpu.VMEM((hb, c, dv + dk), F32), pltpu.VMEM((hb, c, dk), BF16),
                          pltpu.VMEM((hb, c, dk), BF16), pltpu.VMEM((hb, c, dv + dk), F32),
                          pltpu.VMEM((hb, c, dv), BF16), pltpu.VMEM((hb, c, dv), F32)],
        compiler_params=_cparams("arbitrary", "arbitrary", "arbitrary"), name="gdn_seq",
    )(z, z, z, z, w_gconv, w_gconv, w_gconv, zba,
      _group_lanes(a_log).reshape(-1, 1, LANES), _group_lanes(dt_bias).reshape(-1, 1, LANES),
      norm_w.reshape(-1, 1, dv))


def _gdn_dec_body(xq_ref, xk_ref, xv_ref, z_ref, cs_ref, w_ref, b_ref, a_ref, al_ref, dt_ref, nw_ref, s_ref,
                  acc_ref, o_ref, so_ref, cu_ref, ks_ref, qs_ref, *, bb, dk, dv, CH):
    H = GDN_HEADS
    GW = CH // 3
    for gi, x_ref in enumerate((xq_ref, xk_ref, xv_ref)):
        cs = slice(gi * GW, (gi + 1) * GW)
        acc = x_ref[...].astype(F32) * w_ref[GDN_CONV - 1:GDN_CONV, cs]
        for i in range(GDN_CONV - 1):
            acc = acc + cs_ref[:, i * CH + gi * GW:i * CH + (gi + 1) * GW] * w_ref[i:i + 1, cs]
        cu_ref[:, cs] = _silu(acc)
    beta = _sigmoid(b_ref[...])
    eg = jnp.exp(-jnp.exp(al_ref[...]) * _softplus(a_ref[...] + dt_ref[...]))
    nw = nw_ref[...]
    qo, ko, vo = 0, H * dk, 2 * H * dk
    ks, qs = [], []
    for h in range(H):
        q = _l2norm(cu_ref[:, qo + h * dk:qo + (h + 1) * dk]) * (dk ** -0.5)
        k = _l2norm(cu_ref[:, ko + h * dk:ko + (h + 1) * dk])
        ks.append(k)
        qs.append(q)
        for b in range(bb):
            kq = jnp.concatenate([k[b:b + 1], q[b:b + 1], jnp.zeros((6, dk), F32)], axis=0)
            r = _dot(kq.astype(BF16), s_ref[b, h].astype(BF16))
            ks_ref[h, b:b + 1, :] = r[0:1]
            qs_ref[h, b:b + 1, :] = r[1:2]
    vns = []
    for h in range(H):
        q, k = qs[h], ks[h]
        v = cu_ref[:, vo + h * dv:vo + (h + 1) * dv]
        e = eg[:, h:h + 1]
        v_new = beta[:, h:h + 1] * (v - e * ks_ref[h])
        o = e * qs_ref[h] + jnp.sum(q * k, axis=-1, keepdims=True) * v_new
        o = o * lax.rsqrt(jnp.mean(o * o, axis=-1, keepdims=True) + EPS) * nw
        gate = _silu(z_ref[:, h * dv:(h + 1) * dv].astype(F32))
        o_ref[:, h * dv:(h + 1) * dv] = (o * gate).astype(o_ref.dtype)
        vns.append(v_new)
    for h in range(H):
        kT = ks[h].T
        for b in range(bb):
            kc = jnp.broadcast_to(kT[:, b:b + 1], (dk, dv))
            so_ref[b, h] = s_ref[b, h] * eg[b:b + 1, h:h + 1] + kc * vns[h][b:b + 1]


def gdn_dec(z, zb, za, conv_state, w_gconv, a_log, dt_bias, norm_w, state, acc, l, dk, dv, col0, bb=8):
    B = z.shape[0]
    H = GDN_HEADS
    bb = _blk(B, bb)
    CH = 3 * H * dk
    GW = H * dk
    cb0 = col0 // GW
    body = functools.partial(_gdn_dec_body, bb=bb, dk=dk, dv=dv, CH=CH)
    st = pl.BlockSpec((None, bb, H, dk, dv), lambda i: (l, i, 0, 0, 0))
    hrow = pl.BlockSpec((bb, H), lambda i: (i, 0))
    hpar = pl.BlockSpec((None, 1, H), lambda i: (l, 0, 0))
    zspec = lambda off: pl.BlockSpec((bb, GW), lambda i: (i, cb0 + off))
    return pl.pallas_call(
        body, grid=(B // bb,),
        in_specs=[zspec(0), zspec(1), zspec(2), zspec(3),
                  pl.BlockSpec((None, bb, (GDN_CONV - 1) * CH), lambda i: (l, i, 0)),
                  pl.BlockSpec((None, GDN_CONV, CH), lambda i: (l, 0, 0)),
                  hrow, hrow, hpar, hpar,
                  pl.BlockSpec((None, 1, dv), lambda i: (l, 0, 0)), st, pl.BlockSpec(memory_space=pl.ANY)],
        out_specs=[pl.BlockSpec((bb, H * dv), lambda i: (i, 0)), st],
        out_shape=[jax.ShapeDtypeStruct((B, H * dv), BF16), jax.ShapeDtypeStruct(state.shape, F32)],
        input_output_aliases={12: 1},
        scratch_shapes=[pltpu.VMEM((bb, CH), F32), pltpu.VMEM((H, bb, dv), F32), pltpu.VMEM((H, bb, dv), F32)],
        compiler_params=_cparams("parallel"), name="gdn_dec",
    )(z, z, z, z, conv_state.reshape(-1, B, (GDN_CONV - 1) * CH), w_gconv, zb, za,
      a_log.reshape(-1, 1, H), dt_bias.reshape(-1, 1, H), norm_w.reshape(-1, 1, dv), state, acc)


def _softmax_rows(s):
    m = jnp.max(s, axis=-1, keepdims=True)
    e = jnp.exp(s - m)
    return e / jnp.sum(e, axis=-1, keepdims=True)


def _xattn_seq_body(q_ref, k_ref, v_ref, o_ref, *, hd):
    for h in range(X_HEADS):
        sl = slice(h * hd, (h + 1) * hd)
        s = _dot_nt(q_ref[:, sl], k_ref[0, :, sl].astype(BF16)) * (hd ** -0.5)
        a = _softmax_rows(s)
        o_ref[:, sl] = _dot(a.astype(BF16), v_ref[0, :, sl].astype(BF16)).astype(o_ref.dtype)


def xattn_seq(q, mk, mv, B, T, bq=512):
    W = q.shape[1]
    hd = W // X_HEADS
    bq = _blk(T, bq)
    nq = T // bq
    mem = pl.BlockSpec((1,) + mk.shape[1:], lambda b, j: (b, 0, 0))
    return pl.pallas_call(
        functools.partial(_xattn_seq_body, hd=hd), grid=(B, nq),
        in_specs=[pl.BlockSpec((bq, W), lambda b, j: (b * nq + j, 0)), mem, mem],
        out_specs=pl.BlockSpec((bq, W), lambda b, j: (b * nq + j, 0)),
        out_shape=jax.ShapeDtypeStruct((B * T, W), BF16),
        compiler_params=_cparams("parallel", "parallel"), name="xattn_seq",
    )(q, mk, mv)


def _xattn_dec_body(q_ref, k_hbm, v_hbm, o_ref, kbuf, vbuf, sem, *, l, bb, hd, nsteps):
    i = pl.program_id(0)
    W = X_HEADS * hd

    def copies(step, slot):
        out = []
        for t, (src, dst) in enumerate(((k_hbm, kbuf), (v_hbm, vbuf))):
            for h in range(X_HEADS):
                out.append(pltpu.make_async_copy(src.at[l, pl.ds(step * bb, bb), :, h, :],
                                                 dst.at[slot, :, :, pl.ds(h * hd, hd)], sem.at[t, slot, h]))
        return out

    slot = i % 2

    @pl.when(i == 0)
    def _():
        for cp in copies(0, 0):
            cp.start()

    @pl.when(i + 1 < nsteps)
    def _():
        for cp in copies(i + 1, 1 - slot):
            cp.start()

    for cp in copies(i, slot):
        cp.wait()

    q = q_ref[...]
    row = lax.broadcasted_iota(jnp.int32, (8, W), 0)
    head = lax.broadcasted_iota(jnp.int32, (8, W), 1) // hd
    for b in range(bb):
        qd = jnp.where(row == head, jnp.broadcast_to(q[b:b + 1].astype(F32), (8, W)), 0.0).astype(BF16)
        s = _dot_nt(qd, kbuf[slot, b].astype(BF16)) * (hd ** -0.5)
        a = _softmax_rows(s)
        o = _dot(a.astype(BF16), vbuf[slot, b].astype(BF16))
        o_ref[b:b + 1, :] = jnp.sum(jnp.where(row == head, o, 0.0), axis=0, keepdims=True).astype(o_ref.dtype)


def xattn_dec(q, mk, mv, l, bb=8):
    B, W = q.shape
    hd = W // X_HEADS
    n_mem = mk.shape[2]
    bb = _blk(B, bb)
    nsteps = B // bb
    hbm = pl.BlockSpec(memory_space=pl.ANY)
    return pl.pallas_call(
        functools.partial(_xattn_dec_body, l=l, bb=bb, hd=hd, nsteps=nsteps), grid=(nsteps,),
        in_specs=[pl.BlockSpec((bb, W), lambda i: (i, 0)), hbm, hbm],
        out_specs=pl.BlockSpec((bb, W), lambda i: (i, 0)),
        out_shape=jax.ShapeDtypeStruct((B, W), BF16),
        scratch_shapes=[pltpu.VMEM((2, bb, n_mem, W), F32), pltpu.VMEM((2, bb, n_mem, W), F32),
                        pltpu.SemaphoreType.DMA((2, 2, X_HEADS))],
        compiler_params=_cparams("arbitrary"), name="xattn_dec",
    )(q, mk, mv)


def _run_trunk(x, nseq, T, states, mem_k, mem_v, W, depth):
    D = x.shape[1]
    dk_r, dv_r = D // 32, D // 16
    dk_g = dv_g = D // 32
    n_main = W["w_in"].shape[2] - 2 * GDN_HEADS
    pool_w = W["w_pool"].shape[-1] * len(POOL_WINDOWS)
    off_pu = 2 * RET_HEADS * dk_r + 2 * RET_HEADS * dv_r
    off_c = off_pu + pool_w
    ch = 3 * GDN_HEADS * dk_g
    gdn_par = (W["w_gconv"], W["gdn_a_log"], W["gdn_dt_bias"], W["gdn_norm"])
    ffn_par = (W["w_gate"], W["w_up"], W["w_down"], W["w_fconv"], W["b_fconv"])
    outs = [[] for _ in range(5)]
    if states is not None:
        acc_ret, acc_gdn = lax.empty(states[0].shape, F32), lax.empty(states[2].shape, F32)
    hn = rmsnorm(x, W["norm_mix"], 0)
    for l in range(depth):
        z = matmul(hn, W["w_in"], l, out_dtype=BF16, n_out=n_main)
        zba = matmul(hn, W["w_ba"], l, out_dtype=F32)
        if states is None:
            ro, s_ret = retention_seq(z, nseq, T, dk_r, dv_r)
            po = pool_seq(z, W["w_pool"], W["ls_pool"], l, nseq, T, off_pu // pool_w)
            co, s_gdn = gdn_seq(z, zba, *gdn_par, l, nseq, T, dk_g, dv_g, off_c)
            z3 = z.reshape(nseq, T, n_main)
            s_pool = z3[:, T - POOL_BUF:, off_pu:off_pu + pool_w].astype(F32)
            s_gconv = z3[:, T - (GDN_CONV - 1):, off_c:off_c + ch].astype(F32)
        else:
            st_ret, st_pool, st_gdn, st_gconv, st_fconv = states
            zg = zba.reshape(zba.shape[0], GDN_HEADS // GDN_HB, LANES)
            zb = zg[:, :, :GDN_HB].reshape(-1, GDN_HEADS)
            za = zg[:, :, GDN_HB:2 * GDN_HB].reshape(-1, GDN_HEADS)
            ro, acc_ret = retention_dec(z, st_ret, acc_ret, l, dk_r, dv_r)
            po = pool_dec(z, st_pool, W["w_pool"], W["ls_pool"], l, off_pu // pool_w)
            co, acc_gdn = gdn_dec(z, zb, za, st_gconv, *gdn_par, st_gdn, acc_gdn, l, dk_g, dv_g, off_c)
            s_ret = s_gdn = None
            s_pool = jnp.concatenate([st_pool[l, :, 1:], z[:, None, off_pu:off_pu + pool_w].astype(F32)], axis=1)
            s_gconv = jnp.concatenate([st_gconv[l, :, 1:], z[:, None, off_c:off_c + ch].astype(F32)], axis=1)
        x = matmul([ro, po, co], W["w_out"], l, res=x)
        hx = rmsnorm(x, W["norm_x"], l)
        q = matmul(hx, W["w_xq"], l, out_dtype=BF16)
        ao = xattn_seq(q, mem_k[l], mem_v[l], nseq, T) if states is None else xattn_dec(q, mem_k, mem_v, l)
        x = matmul(ao, W["w_xo"], l, res=x)
        hf = rmsnorm(x, W["norm_ffn"], l)
        if states is None:
            y, s_fconv = ffn_seq(hf, *ffn_par, l, nseq)
        else:
            y, g_new = ffn_dec(hf, *ffn_par, st_fconv, l)
            s_fconv = jnp.concatenate([st_fconv[l, :, 1:], g_new[:, None, :]], axis=1)
        if l + 1 < depth:
            x, hn = add_rmsnorm(x, y, W["norm_mix"], l + 1)
        else:
            x, hn = add_rmsnorm(x, y, W["norm_f"], 0, out_dtype=F32)
        for lst, s in zip(outs, (s_ret, s_pool, s_gdn, s_gconv, s_fconv)):
            lst.append(s)
    if states is None:
        return hn, [jnp.stack(s) for s in outs]
    return hn, [acc_ret, jnp.stack(outs[1]), acc_gdn, jnp.stack(outs[3]), jnp.stack(outs[4])]


def kernel(x_prompt, x_sample, state_ret, state_pool, state_gdn, state_gdn_conv, state_ffn_conv, cache_mem_k, cache_mem_v, mem_prompt, norm_mix, w_in, w_pool, ls_pool, w_gconv, gdn_a_log, gdn_dt_bias, gdn_norm, w_out, norm_x, norm_mem, w_xq, w_xk, w_xv, w_xo, norm_ffn, w_gate, w_up, w_fconv, b_fconv, w_down, norm_f):
    depth = w_in.shape[0]
    Bp, T, D = x_prompt.shape
    Bs = x_sample.shape[0]
    assert x_sample.shape[1] == 1 and D % (32 * LANES) == 0
    n_in = w_in.shape[2]
    n_main = n_in - 2 * GDN_HEADS
    hg = GDN_HEADS // GDN_HB
    w_b = w_in[:, :, n_main:n_main + GDN_HEADS].reshape(depth, D, hg, GDN_HB)
    w_a = w_in[:, :, n_main + GDN_HEADS:].reshape(depth, D, hg, GDN_HB)
    w_ba = jnp.pad(jnp.concatenate([w_b, w_a], axis=-1), ((0, 0), (0, 0), (0, 0), (0, LANES - 2 * GDN_HB)))
    w_ba = w_ba.reshape(depth, D, hg * LANES)
    W = dict(norm_mix=norm_mix, norm_x=norm_x, norm_ffn=norm_ffn, norm_f=norm_f.reshape(1, D),
             w_in=w_in.astype(BF16), w_ba=w_ba.astype(BF16),
             w_pool=w_pool.astype(BF16), ls_pool=ls_pool, w_gconv=w_gconv,
             gdn_a_log=gdn_a_log, gdn_dt_bias=gdn_dt_bias, gdn_norm=gdn_norm,
             w_out=w_out.astype(BF16), w_xq=w_xq.astype(BF16), w_xo=w_xo.astype(BF16),
             w_gate=w_gate.astype(BF16), w_up=w_up.astype(BF16), w_down=w_down.astype(BF16),
             w_fconv=w_fconv, b_fconv=b_fconv)
    n_mem = mem_prompt.shape[1]
    memf = mem_prompt.reshape(Bp * n_mem, D)
    w_xk16, w_xv16 = w_xk.astype(BF16), w_xv.astype(BF16)
    pk, pv = [], []
    for l in range(depth):
        mn = rmsnorm(memf, norm_mem, l)
        pk.append(matmul(mn, w_xk16, l).reshape(Bp, n_mem, -1))
        pv.append(matmul(mn, w_xv16, l).reshape(Bp, n_mem, -1))
    y_p, p_st = _run_trunk(x_prompt.reshape(Bp * T, D), Bp, T, None, pk, pv, W, depth)
    y_s, s_st = _run_trunk(x_sample.reshape(Bs, D), Bs, 1,
                           (state_ret, state_pool, state_gdn, state_gdn_conv, state_ffn_conv),
                           cache_mem_k, cache_mem_v, W, depth)
    p_mem_k = jnp.stack(pk).reshape(depth, Bp, n_mem, X_HEADS, -1)
    p_mem_v = jnp.stack(pv).reshape(depth, Bp, n_mem, X_HEADS, -1)
    return (y_p.reshape(Bp, T, D), y_s.reshape(Bs, 1, D), *p_st, p_mem_k, p_mem_v, *s_st)
```

```python
import functools
import math

import jax
import jax.numpy as jnp
from jax import lax
from jax.experimental import pallas as pl
from jax.experimental.pallas import tpu as pltpu

F32 = jnp.float32
BF16 = jnp.bfloat16

EPS = 1e-6
ROPE_THETA = 10000.0
RET_HEADS = 4
GDN_HEADS = 16
X_HEADS = 4
POOL_WINDOWS = (2, 4, 8, 16)
POOL_BUF = max(POOL_WINDOWS) - 1
GDN_CONV = 4
FFN_CONV = 3
PAST_LEN = 16384

V7X_VMEM_BYTES = 64 * 1024 * 1024
VMEM_LIMIT = V7X_VMEM_BYTES - 8 * 1024 * 1024
LANES = 128
GDN_CHUNK = 64
GDN_SUB = 8
GDN_HB = 16
GDN_SOLVE = 16
NEG_BIG = -1e30


def _cparams(*sem):
    return pltpu.CompilerParams(dimension_semantics=sem, vmem_limit_bytes=VMEM_LIMIT)


def _dot(a, b):
    return jnp.dot(a, b, preferred_element_type=F32)


def _dot_nt(a, b):
    return lax.dot_general(a, b, (((1,), (1,)), ((), ())), preferred_element_type=F32)


def _dot_tn(a, b):
    return lax.dot_general(a, b, (((0,), (0,)), ((), ())), preferred_element_type=F32)


def _silu(x):
    return (0.5 * x) * (1.0 + jnp.tanh(0.5 * x))


def _blk(n, want):
    for step in (16, 8):
        for b in range(min(n, want) // step * step, 0, -step):
            if n % b == 0:
                return b
    return n


def _rmsnorm_body(x_ref, g_ref, o_ref):
    x = x_ref[...]
    y = x * lax.rsqrt(jnp.mean(x * x, axis=-1, keepdims=True) + EPS)
    o_ref[...] = (y * g_ref[...]).astype(o_ref.dtype)


def _gain_spec(l, D):
    return pl.BlockSpec((None, 1, D), lambda i: (l, 0, 0))


def rmsnorm(x, g, l, out_dtype=BF16):
    M, D = x.shape
    bm = _blk(M, 256)
    return pl.pallas_call(
        _rmsnorm_body, grid=(M // bm,),
        in_specs=[pl.BlockSpec((bm, D), lambda i: (i, 0)), _gain_spec(l, D)],
        out_specs=pl.BlockSpec((bm, D), lambda i: (i, 0)),
        out_shape=jax.ShapeDtypeStruct((M, D), out_dtype),
        compiler_params=_cparams("parallel"), name="rmsnorm",
    )(x, g.reshape(-1, 1, D))


def _add_rmsnorm_body(x_ref, y_ref, g_ref, s_ref, o_ref):
    x = x_ref[...] + y_ref[...]
    s_ref[...] = x
    y = x * lax.rsqrt(jnp.mean(x * x, axis=-1, keepdims=True) + EPS)
    o_ref[...] = (y * g_ref[...]).astype(o_ref.dtype)


def add_rmsnorm(x, y, g, l, out_dtype=BF16):
    M, D = x.shape
    bm = _blk(M, 256)
    row = pl.BlockSpec((bm, D), lambda i: (i, 0))
    return pl.pallas_call(
        _add_rmsnorm_body, grid=(M // bm,),
        in_specs=[row, row, _gain_spec(l, D)],
        out_specs=[row, row],
        out_shape=[jax.ShapeDtypeStruct((M, D), F32), jax.ShapeDtypeStruct((M, D), out_dtype)],
        compiler_params=_cparams("parallel"), name="add_rmsnorm",
    )(x, y, g.reshape(-1, 1, D))


def _mm_body(*refs, widths, has_res):
    xs, w_ref = refs[:len(widths)], refs[len(widths)]
    o_ref = refs[-1]
    acc = refs[len(widths) + 1][...] if has_res else None
    k0 = 0
    for x_ref, kw in zip(xs, widths):
        d = _dot(x_ref[...], w_ref[k0:k0 + kw, :])
        acc = d if acc is None else acc + d
        k0 += kw
    o_ref[...] = acc.astype(o_ref.dtype)


def matmul(xs, w, l, res=None, out_dtype=F32, n_out=None, bm=1024, bn=1024):
    xs = list(xs) if isinstance(xs, (list, tuple)) else [xs]
    M = xs[0].shape[0]
    widths = tuple(x.shape[1] for x in xs)
    K, N = w.shape[1], n_out or w.shape[2]
    assert sum(widths) == K
    bm, bn = _blk(M, bm), _blk(N, bn)
    in_specs = [pl.BlockSpec((bm, kw), lambda i, j: (i, 0)) for kw in widths]
    in_specs.append(pl.BlockSpec((None, K, bn), lambda i, j: (l, 0, j)))
    args = xs + [w]
    if res is not None:
        in_specs.append(pl.BlockSpec((bm, bn), lambda i, j: (i, j)))
        args.append(res)
    return pl.pallas_call(
        functools.partial(_mm_body, widths=widths, has_res=res is not None),
        grid=(M // bm, N // bn), in_specs=in_specs,
        out_specs=pl.BlockSpec((bm, bn), lambda i, j: (i, j)),
        out_shape=jax.ShapeDtypeStruct((M, N), out_dtype),
        compiler_params=_cparams("parallel", "parallel"), name="matmul",
    )(*args)


def _ffn_act(a, u):
    return (0.5 * a * (1.0 + lax.erf(a * (2.0 ** -0.5))) * u).astype(BF16)


def _ffn_seq_body(h_ref, wg_ref, wu_ref, wd_ref, cw_ref, cb_ref, o_ref, fst_ref, tail_ref, act_ref, *,
                  bm, blocks_per_seq, nf):
    i = pl.program_id(0)
    f = pl.program_id(1)

    @pl.when(f == 0)
    def _():
        act_ref[...] = jnp.zeros(act_ref.shape, act_ref.dtype)
        o_ref[...] = jnp.zeros(o_ref.shape, o_ref.dtype)

    @pl.when((i % blocks_per_seq == 0) & (f < nf))
    def _():
        tail_ref[f] = jnp.zeros(tail_ref.shape[1:], F32)

    @pl.when(f == nf)
    def _():
        o_ref[...] += _dot(act_ref[...], wd_ref[...])

    @pl.when(f < nf)
    def _():
        o_ref[...] += _dot(act_ref[...], wd_ref[...])
        h = h_ref[...]
        g = _dot(h, wg_ref[...])
        u = _dot(h, wu_ref[...])
        prev = tail_ref[f]
        p2, p1 = prev[0:1, :], prev[1:2, :]
        row = lax.broadcasted_iota(jnp.int32, g.shape, 0)
        s1 = jnp.where(row == 0, p1, pltpu.roll(g, 1, 0))
        s2 = jnp.where(row == 0, p2, jnp.where(row == 1, p1, pltpu.roll(g, 2, 0)))
        cw = cw_ref[...]
        a = s2 * cw[0:1, :] + s1 * cw[1:2, :] + g * cw[2:3, :] + cb_ref[...]
        last2 = g[bm - 2:bm, :]
        tail_ref[f, 0:2, :] = last2
        fst_ref[0, f] = last2
        act_ref[...] = _ffn_act(a, u)


def ffn_seq(h, wg, wu, wd, cw, cb, l, nseq, bm=1024, bf=256):
    M, D = h.shape
    F = wg.shape[2]
    T = M // nseq
    bm = _blk(T, bm)
    nf = F // bf
    bps = T // bm
    body = functools.partial(_ffn_seq_body, bm=bm, blocks_per_seq=bps, nf=nf)
    cur = lambda f: jnp.minimum(f, nf - 1)
    y, fst = pl.pallas_call(
        body, grid=(M // bm, nf + 1),
        in_specs=[pl.BlockSpec((bm, D), lambda i, f: (i, 0)),
                  pl.BlockSpec((None, D, bf), lambda i, f: (l, 0, cur(f))),
                  pl.BlockSpec((None, D, bf), lambda i, f: (l, 0, cur(f))),
                  pl.BlockSpec((None, bf, D), lambda i, f: (l, jnp.maximum(f - 1, 0), 0)),
                  pl.BlockSpec((None, FFN_CONV, bf), lambda i, f: (l, 0, cur(f))),
                  pl.BlockSpec((None, 1, bf), lambda i, f: (l, 0, cur(f)))],
        out_specs=[pl.BlockSpec((bm, D), lambda i, f: (i, 0), pipeline_mode=pl.Buffered(1)),
                   pl.BlockSpec((1, nf, 2, bf), lambda i, f: (i // bps, 0, 0, 0))],
        out_shape=[jax.ShapeDtypeStruct((M, D), F32), jax.ShapeDtypeStruct((nseq, nf, 2, bf), F32)],
        scratch_shapes=[pltpu.VMEM((nf, 8, bf), F32), pltpu.VMEM((bm, bf), BF16)],
        compiler_params=_cparams("arbitrary", "arbitrary"), name="ffn_seq",
    )(h, wg, wu, wd, cw, cb.reshape(cb.shape[0], 1, F))
    return y, fst.transpose(0, 2, 1, 3).reshape(nseq, 2, F)


def _ffn_dec_body(h_ref, wg_ref, wu_ref, wd_ref, cw_ref, cb_ref, s0_ref, s1_ref, o_ref, g_ref):
    f = pl.program_id(0)
    h = h_ref[...]
    g = _dot(h, wg_ref[...])
    u = _dot(h, wu_ref[...])
    g_ref[...] = g
    cw = cw_ref[...]
    a = s0_ref[...] * cw[0:1, :] + s1_ref[...] * cw[1:2, :] + g * cw[2:3, :] + cb_ref[...]
    d = _dot(_ffn_act(a, u), wd_ref[...])

    @pl.when(f == 0)
    def _():
        o_ref[...] = d

    @pl.when(f != 0)
    def _():
        o_ref[...] += d


def ffn_dec(h, wg, wu, wd, cw, cb, st, l, bf=256):
    B, D = h.shape
    F = wg.shape[2]
    nf = F // bf
    st2 = st.reshape(st.shape[0], B, 2 * F)
    return pl.pallas_call(
        _ffn_dec_body, grid=(nf,),
        in_specs=[pl.BlockSpec((B, D), lambda f: (0, 0)),
                  pl.BlockSpec((None, D, bf), lambda f: (l, 0, f)),
                  pl.BlockSpec((None, D, bf), lambda f: (l, 0, f)),
                  pl.BlockSpec((None, bf, D), lambda f: (l, f, 0)),
                  pl.BlockSpec((None, FFN_CONV, bf), lambda f: (l, 0, f)),
                  pl.BlockSpec((None, 1, bf), lambda f: (l, 0, f)),
                  pl.BlockSpec((None, B, bf), lambda f: (l, 0, f)),
                  pl.BlockSpec((None, B, bf), lambda f: (l, 0, nf + f))],
        out_specs=[pl.BlockSpec((B, D), lambda f: (0, 0)), pl.BlockSpec((B, bf), lambda f: (0, f))],
        out_shape=[jax.ShapeDtypeStruct((B, D), F32), jax.ShapeDtypeStruct((B, F), F32)],
        compiler_params=_cparams("arbitrary"), name="ffn_dec",
    )(h, wg, wu, wd, cw, cb.reshape(cb.shape[0], 1, F), st2, st2)


def _log_gamma(h):
    return math.log(1.0 - 2.0 ** (-5.0 - h))


def _rope_tables(pos, half):
    inv = ROPE_THETA ** (-jnp.arange(half, dtype=F32) / half)
    ang = pos.astype(F32)[:, None] * inv[None, :]
    cos, sin = jnp.cos(ang), jnp.sin(ang)
    return jnp.concatenate([cos, cos], axis=-1), jnp.concatenate([-sin, sin], axis=-1)


def _rope(x, cosf, sins):
    return x * cosf + pltpu.roll(x, x.shape[-1] // 2, 1) * sins


def _ret_seq_body(q_ref, k_ref, v_ref, g_ref, cos_ref, sin_ref, o_ref, so_ref, s_ref, *, C, dk, dv):
    j = pl.program_id(1)

    @pl.when(j == 0)
    def _():
        s_ref[...] = jnp.zeros(s_ref.shape, F32)

    cosf, sins = cos_ref[...], sin_ref[...]
    ri = lax.broadcasted_iota(jnp.int32, (C, C), 0)
    ci = lax.broadcasted_iota(jnp.int32, (C, C), 1)
    idx = lax.broadcasted_iota(jnp.int32, (C, 1), 0).astype(F32)
    for h in range(RET_HEADS):
        lg = _log_gamma(h)
        q = _rope(q_ref[:, h * dk:(h + 1) * dk].astype(F32), cosf, sins)
        k = _rope(k_ref[:, h * dk:(h + 1) * dk].astype(F32), cosf, sins) * (dk ** -0.5)
        v = v_ref[:, h * dv:(h + 1) * dv]
        dmask = jnp.exp(jnp.where(ri >= ci, (ri - ci).astype(F32) * lg, NEG_BIG))
        scores = _dot_nt(q.astype(BF16), k.astype(BF16)) * dmask
        S = s_ref[h]
        qd = q * jnp.exp((idx + 1.0) * lg)
        o = _dot(scores.astype(BF16), v) + _dot(qd.astype(BF16), S.astype(BF16))
        kd = k * jnp.exp((C - 1.0 - idx) * lg)
        s_ref[h] = S * math.exp(C * lg) + _dot_tn(kd.astype(BF16), v)
        o = o * lax.rsqrt(jnp.mean(o * o, axis=-1, keepdims=True) + EPS)
        o_ref[:, h * dv:(h + 1) * dv] = (o * _silu(g_ref[:, h * dv:(h + 1) * dv].astype(F32))).astype(o_ref.dtype)

    @pl.when(j == pl.num_programs(1) - 1)
    def _():
        so_ref[0] = s_ref[...]


def retention_seq(z, B, T, dk, dv, C=256):
    C = _blk(T, C)
    nc = T // C
    qw, vw = RET_HEADS * dk, RET_HEADS * dv
    cosf, sins = _rope_tables(jnp.arange(T), dk // 2)
    body = functools.partial(_ret_seq_body, C=C, dk=dk, dv=dv)
    rows = lambda b, j: b * nc + j
    return pl.pallas_call(
        body, grid=(B, nc),
        in_specs=[pl.BlockSpec((C, qw), lambda b, j: (rows(b, j), 0)),
                  pl.BlockSpec((C, qw), lambda b, j: (rows(b, j), 1)),
                  pl.BlockSpec((C, vw), lambda b, j: (rows(b, j), (2 * qw) // vw)),
                  pl.BlockSpec((C, vw), lambda b, j: (rows(b, j), (2 * qw) // vw + 1)),
                  pl.BlockSpec((C, dk), lambda b, j: (j, 0)),
                  pl.BlockSpec((C, dk), lambda b, j: (j, 0))],
        out_specs=[pl.BlockSpec((C, vw), lambda b, j: (rows(b, j), 0)),
                   pl.BlockSpec((1, RET_HEADS, dk, dv), lambda b, j: (b, 0, 0, 0))],
        out_shape=[jax.ShapeDtypeStruct((B * T, vw), BF16), jax.ShapeDtypeStruct((B, RET_HEADS, dk, dv), F32)],
        scratch_shapes=[pltpu.VMEM((RET_HEADS, dk, dv), F32)],
        compiler_params=_cparams("arbitrary", "arbitrary"), name="retention_seq",
    )(z, z, z, z, cosf, sins)


def _row0(x, rows=8):
    r = lax.broadcasted_iota(jnp.int32, (rows, x.shape[1]), 0)
    return jnp.where(r == 0, jnp.broadcast_to(x, (rows, x.shape[1])), 0.0)


def _ret_dec_body(q_ref, k_ref, v_ref, g_ref, cos_ref, sin_ref, s_ref, acc_ref, o_ref, so_ref, qs_ref, *,
                  bb, dk, dv):
    cosf, sins = cos_ref[...], sin_ref[...]
    qs, ks = [], []
    for h in range(RET_HEADS):
        q = _rope(q_ref[:, h * dk:(h + 1) * dk].astype(F32), cosf, sins)
        k = _rope(k_ref[:, h * dk:(h + 1) * dk].astype(F32), cosf, sins) * (dk ** -0.5)
        qs.append(q)
        ks.append(k)
        for b in range(bb):
            r = _dot(_row0(q[b:b + 1]).astype(BF16), s_ref[b, h].astype(BF16))
            qs_ref[h, b:b + 1, :] = r[0:1]
    for h in range(RET_HEADS):
        gamma = math.exp(_log_gamma(h))
        v = v_ref[:, h * dv:(h + 1) * dv].astype(F32)
        o = gamma * qs_ref[h] + jnp.sum(qs[h] * ks[h], axis=-1, keepdims=True) * v
        o = o * lax.rsqrt(jnp.mean(o * o, axis=-1, keepdims=True) + EPS)
        gate = _silu(g_ref[:, h * dv:(h + 1) * dv].astype(F32))
        o_ref[:, h * dv:(h + 1) * dv] = (o * gate).astype(o_ref.dtype)
    for h in range(RET_HEADS):
        gamma = math.exp(_log_gamma(h))
        kT = ks[h].T
        v = v_ref[:, h * dv:(h + 1) * dv].astype(F32)
        for b in range(bb):
            kc = jnp.broadcast_to(kT[:, b:b + 1], (dk, dv))
            so_ref[b, h] = s_ref[b, h] * gamma + kc * v[b:b + 1]


def retention_dec(z, state, acc, l, dk, dv, bb=8):
    B = z.shape[0]
    bb = _blk(B, bb)
    qw, vw = RET_HEADS * dk, RET_HEADS * dv
    cosf, sins = _rope_tables(jnp.full((1,), PAST_LEN), dk // 2)
    body = functools.partial(_ret_dec_body, bb=bb, dk=dk, dv=dv)
    st = pl.BlockSpec((None, bb, RET_HEADS, dk, dv), lambda i: (l, i, 0, 0, 0))
    return pl.pallas_call(
        body, grid=(B // bb,),
        in_specs=[pl.BlockSpec((bb, qw), lambda i: (i, 0)),
                  pl.BlockSpec((bb, qw), lambda i: (i, 1)),
                  pl.BlockSpec((bb, vw), lambda i: (i, (2 * qw) // vw)),
                  pl.BlockSpec((bb, vw), lambda i: (i, (2 * qw) // vw + 1)),
                  pl.BlockSpec((1, dk), lambda i: (0, 0)),
                  pl.BlockSpec((1, dk), lambda i: (0, 0)), st, pl.BlockSpec(memory_space=pl.ANY)],
        out_specs=[pl.BlockSpec((bb, vw), lambda i: (i, 0)), st],
        out_shape=[jax.ShapeDtypeStruct((B, vw), BF16), jax.ShapeDtypeStruct(state.shape, F32)],
        input_output_aliases={7: 1},
        scratch_shapes=[pltpu.VMEM((RET_HEADS, bb, dv), F32)],
        compiler_params=_cparams("parallel"), name="retention_dec",
    )(z, z, z, z, cosf, sins, state, acc)


def _pool_seq_body(u_ref, w_ref, ls_ref, o_ref, ext_ref, *, R, gw):
    j = pl.program_id(1)

    @pl.when(j == 0)
    def _():
        ext_ref[0:16, :] = jnp.zeros((16, ext_ref.shape[1]), F32)

    ext_ref[16:16 + R, :] = u_ref[...].astype(F32)
    pos = j * R + lax.broadcasted_iota(jnp.int32, (R, 1), 0)
    for gi, win in enumerate(POOL_WINDOWS):
        sl = slice(gi * gw, (gi + 1) * gw)
        cur = ext_ref[16:16 + R, sl]
        acc = cur
        for s in range(1, win):
            acc = acc + ext_ref[16 - s:16 - s + R, sl]
        cnt = jnp.minimum(pos + 1, win).astype(F32)
        d = acc / cnt - cur
        o_ref[:, sl] = (_dot(d.astype(BF16), w_ref[gi]) * ls_ref[:, sl]).astype(o_ref.dtype)
    ext_ref[0:16, :] = ext_ref[R:R + 16, :]


def pool_seq(z, w_pool, ls, l, B, T, col_block, R=256):
    R = _blk(T, R)
    nr = T // R
    gw = w_pool.shape[-1]
    W = gw * len(POOL_WINDOWS)
    body = functools.partial(_pool_seq_body, R=R, gw=gw)
    return pl.pallas_call(
        body, grid=(B, nr),
        in_specs=[pl.BlockSpec((R, W), lambda b, j: (b * nr + j, col_block)),
                  pl.BlockSpec((None,) + w_pool.shape[1:], lambda b, j: (l, 0, 0, 0)),
                  pl.BlockSpec((None, 1, W), lambda b, j: (l, 0, 0))],
        out_specs=pl.BlockSpec((R, W), lambda b, j: (b * nr + j, 0)),
        out_shape=jax.ShapeDtypeStruct((B * T, W), BF16),
        scratch_shapes=[pltpu.VMEM((16 + R, W), F32)],
        compiler_params=_cparams("arbitrary", "arbitrary"), name="pool_seq",
    )(z, w_pool, ls.reshape(-1, 1, W))


def _pool_dec_body(u_ref, st_ref, w_ref, ls_ref, o_ref, *, gw, W):
    for gi, win in enumerate(POOL_WINDOWS):
        sl = slice(gi * gw, (gi + 1) * gw)
        cur = u_ref[:, sl].astype(F32)
        acc = cur
        for s in range(1, win):
            r = POOL_BUF - s
            acc = acc + st_ref[:, r * W + gi * gw:r * W + (gi + 1) * gw]
        cnt = float(min(PAST_LEN + 1, win))
        d = acc / cnt - cur
        o_ref[:, sl] = (_dot(d.astype(BF16), w_ref[gi]) * ls_ref[:, sl]).astype(o_ref.dtype)


def pool_dec(z, state, w_pool, ls, l, col_block, bb=64):
    B = z.shape[0]
    bb = _blk(B, bb)
    gw = w_pool.shape[-1]
    W = gw * len(POOL_WINDOWS)
    body = functools.partial(_pool_dec_body, gw=gw, W=W)
    return pl.pallas_call(
        body, grid=(B // bb,),
        in_specs=[pl.BlockSpec((bb, W), lambda i: (i, col_block)),
                  pl.BlockSpec((None, bb, POOL_BUF * W), lambda i: (l, i, 0)),
                  pl.BlockSpec((None,) + w_pool.shape[1:], lambda i: (l, 0, 0, 0)),
                  pl.BlockSpec((None, 1, W), lambda i: (l, 0, 0))],
        out_specs=pl.BlockSpec((bb, W), lambda i: (i, 0)),
        out_shape=jax.ShapeDtypeStruct((B, W), BF16),
        compiler_params=_cparams("parallel"), name="pool_dec",
    )(z, state.reshape(-1, B, POOL_BUF * W), w_pool, ls.reshape(-1, 1, W))


def _l2norm(x):
    return x * lax.rsqrt(jnp.sum(x * x, axis=-1, keepdims=True) + EPS)


def _softplus(x):
    return jnp.maximum(x, 0.0) + jnp.log1p(jnp.exp(-jnp.abs(x)))


def _sigmoid(x):
    return 1.0 / (1.0 + jnp.exp(-x))


def _solve_unit_lower(Ls, R, W):
    c = Ls[0].shape[0]
    n = len(Ls)
    blocks = []
    for a in range(c // GDN_SUB):
        r0 = a * GDN_SUB
        Rb = R[r0:r0 + GDN_SUB, :]
        if a:
            Xp = jnp.concatenate(blocks + [jnp.zeros((c - r0, n * W), F32)], axis=0).astype(BF16)
            Rb = Rb - jnp.concatenate(
                [_dot(Ls[h][r0:r0 + GDN_SUB, :].astype(BF16), Xp[:, h * W:(h + 1) * W]) for h in range(n)], axis=1)
        for jj in range(GDN_SUB - 1):
            coef = jnp.concatenate(
                [jnp.broadcast_to(Ls[h][r0:r0 + GDN_SUB, r0 + jj:r0 + jj + 1], (GDN_SUB, W)) for h in range(n)], axis=1)
            Rb = Rb - coef * Rb[jj:jj + 1]
        blocks.append(Rb)
    return jnp.concatenate(blocks, axis=0)


def _gdn_seq_body(q_ref, k_ref, v_ref, z_ref, wq_ref, wk_ref, wv_ref, ba_ref, al_ref, dt_ref, nw_ref,
                  o_ref, so_ref, s_ref, eq_ref, ek_ref, ev_ref, cq_ref, ck_ref, cv_ref,
                  l_s, at_s, rhs_s, qe_s, kd_s, uw_s, vn_s, op_s, *, R, dk, dv):
    r = pl.program_id(2)
    c = GDN_CHUNK
    hb = GDN_HB

    @pl.when(r == 0)
    def _():
        s_ref[...] = jnp.zeros(s_ref.shape, F32)
        for e in (eq_ref, ek_ref, ev_ref):
            e[...] = jnp.zeros(e.shape, F32)

    row8 = lax.broadcasted_iota(jnp.int32, eq_ref.shape, 0)
    for x_ref, w_ref, e_ref, c_ref in ((q_ref, wq_ref, eq_ref, cq_ref), (k_ref, wk_ref, ek_ref, ck_ref),
                                       (v_ref, wv_ref, ev_ref, cv_ref)):
        xf = x_ref[...].astype(F32)
        prev = e_ref[...]
        w = w_ref[...]
        acc = xf * w[GDN_CONV - 1:GDN_CONV, :]
        for s in range(1, GDN_CONV):
            xs = pltpu.roll(xf, s, 0)
            top = jnp.where(row8 < s, pltpu.roll(prev, s, 0), xs[0:8])
            xs = jnp.concatenate([top, xs[8:]], axis=0)
            acc = acc + xs * w[GDN_CONV - 1 - s:GDN_CONV - s, :]
        c_ref[...] = _silu(acc)
        e_ref[...] = xf[R - 8:R]

    ri = lax.broadcasted_iota(jnp.int32, (c, c), 0)
    ci = lax.broadcasted_iota(jnp.int32, (c, c), 1)
    tril = (ri >= ci).astype(F32)
    triu = (ri <= ci).astype(F32)
    neg_a = -jnp.exp(al_ref[0])
    dtb = dt_ref[0]
    nw = nw_ref[...]

    def chunk(ic, carry):
        r0 = pl.multiple_of(ic * c, c)
        rows = pl.ds(r0, c)
        x = ba_ref[rows, :]
        beta_all = _sigmoid(x)
        g_all = neg_a * _softplus(x + dtb)
        b_col = jnp.dot(tril, g_all, precision=lax.Precision.HIGHEST, preferred_element_type=F32)
        b_row = lax.dot_general(g_all, triu, (((0,), (0,)), ((), ())), precision=lax.Precision.HIGHEST,
                                preferred_element_type=F32)
        W = dv + dk
        for hh in range(hb):
            hs = slice(hh * dk, (hh + 1) * dk)
            q = _l2norm(cq_ref[rows, hs]) * (dk ** -0.5)
            k = _l2norm(ck_ref[rows, hs])
            v = cv_ref[rows, hh * dv:(hh + 1) * dv]
            bc = jnp.broadcast_to(b_col[:, hb + hh:hb + hh + 1], (c, dk))
            br = b_row[hb + hh:hb + hh + 1, :]
            beta = jnp.broadcast_to(beta_all[:, hh:hh + 1], (c, dk))
            decay = jnp.exp(jnp.where(ri >= ci, bc[:, :c] - br, NEG_BIG))
            kb = k * beta
            kbf = k.astype(BF16)
            l_s[hh] = jnp.where(ri > ci, _dot_nt(kb.astype(BF16), kbf) * decay, 0.0)
            at_s[hh] = (_dot_nt(q.astype(BF16), kbf) * decay).astype(BF16)
            eb = jnp.exp(bc)
            rhs_s[hh] = jnp.concatenate([v * beta, kb * eb], axis=1)
            qe_s[hh] = (q * eb).astype(BF16)
            kd_s[hh] = (k * jnp.exp(bc[c - 1:c, :] - bc)).astype(BF16)
        for g0 in range(0, hb, GDN_SOLVE):
            uw = _solve_unit_lower([l_s[h] for h in range(g0, g0 + GDN_SOLVE)],
                                   jnp.concatenate([rhs_s[h] for h in range(g0, g0 + GDN_SOLVE)], axis=1), W)
            for h in range(g0, g0 + GDN_SOLVE):
                uw_s[h] = uw[:, (h - g0) * W:(h - g0 + 1) * W]
        for hh in range(hb):
            Sb = s_ref[hh].astype(BF16)
            ws = _dot(jnp.concatenate([uw_s[hh, :, dv:].astype(BF16), qe_s[hh]], axis=0), Sb)
            vn_s[hh] = (uw_s[hh, :, :dv] - ws[:c]).astype(BF16)
            op_s[hh] = ws[c:]
        for hh in range(hb):
            vnb = vn_s[hh]
            o = op_s[hh] + _dot(at_s[hh], vnb)
            ebl = jnp.exp(b_col[c - 1:c, hb + hh:hb + hh + 1])
            s_ref[hh] = s_ref[hh] * ebl + _dot_tn(kd_s[hh], vnb)
            o = o * lax.rsqrt(jnp.mean(o * o, axis=-1, keepdims=True) + EPS) * nw
            gate = _silu(z_ref[rows, hh * dv:(hh + 1) * dv].astype(F32))
            o_ref[rows, hh * dv:(hh + 1) * dv] = (o * gate).astype(o_ref.dtype)
        return carry

    lax.fori_loop(0, R // c, chunk, 0)

    @pl.when(r == pl.num_programs(2) - 1)
    def _():
        so_ref[0] = s_ref[...]


def _group_lanes(t):
    hb, HG = GDN_HB, GDN_HEADS // GDN_HB
    t = t.reshape(t.shape[:-1] + (HG, hb))
    t = jnp.pad(t, [(0, 0)] * (t.ndim - 1) + [(hb, LANES - 2 * hb)])
    return t.reshape(t.shape[:-2] + (HG * LANES,))


def gdn_seq(z, zba, w_gconv, a_log, dt_bias, norm_w, l, B, T, dk, dv, col0, R=256):
    H, hb, c = GDN_HEADS, GDN_HB, GDN_CHUNK
    HG = H // hb
    R = _blk(T, R)
    nr = T // R
    M = B * T
    bw = hb * dk
    cb0 = col0 // bw
    per = (H * dk) // bw
    body = functools.partial(_gdn_seq_body, R=R, dk=dk, dv=dv)
    rows = lambda b, g, r: b * nr + r
    zspec = lambda off: pl.BlockSpec((R, bw), lambda b, g, r: (rows(b, g, r), cb0 + off + g))
    wspec = lambda off: pl.BlockSpec((None, GDN_CONV, bw), lambda b, g, r: (l, 0, off + g))
    pspec = pl.BlockSpec((1, 1, LANES), lambda b, g, r: (l * HG + g, 0, 0))
    return pl.pallas_call(
        body, grid=(B, HG, nr),
        in_specs=[zspec(0), zspec(per), zspec(2 * per), zspec(3 * per),
                  wspec(0), wspec(per), wspec(2 * per),
                  pl.BlockSpec((R, LANES), lambda b, g, r: (rows(b, g, r), g)), pspec, pspec,
                  pl.BlockSpec((None, 1, dv), lambda b, g, r: (l, 0, 0))],
        out_specs=[pl.BlockSpec((R, bw), lambda b, g, r: (rows(b, g, r), g)),
                   pl.BlockSpec((1, hb, dk, dv), lambda b, g, r: (b, g, 0, 0))],
        out_shape=[jax.ShapeDtypeStruct((M, H * dv), BF16), jax.ShapeDtypeStruct((B, H, dk, dv), F32)],
        scratch_shapes=[pltpu.VMEM((hb, dk, dv), F32)] + [pltpu.VMEM((8, bw), F32)] * 3
                       + [pltpu.VMEM((R, bw), F32)] * 3
                       + [pltpu.VMEM((hb, c, c), F32), pltpu.VMEM((hb, c, c), BF16),
                          pltpu.VMEM((hb, c, dv + dk), F32), pltpu.VMEM((hb, c, dk), BF16),
                          pltpu.VMEM((hb, c, dk), BF16), pltpu.VMEM((hb, c, dv + dk), F32),
                          pltpu.VMEM((hb, c, dv), BF16), pltpu.VMEM((hb, c, dv), F32)],
        compiler_params=_cparams("arbitrary", "arbitrary", "arbitrary"), name="gdn_seq",
    )(z, z, z, z, w_gconv, w_gconv, w_gconv, zba,
      _group_lanes(a_log).reshape(-1, 1, LANES), _group_lanes(dt_bias).reshape(-1, 1, LANES),
      norm_w.reshape(-1, 1, dv))


def _gdn_dec_body(xq_ref, xk_ref, xv_ref, z_ref, cs_ref, w_ref, b_ref, a_ref, al_ref, dt_ref, nw_ref, s_ref,
                  acc_ref, o_ref, so_ref, cu_ref, ks_ref, qs_ref, *, bb, dk, dv, CH):
    H = GDN_HEADS
    GW = CH // 3
    for gi, x_ref in enumerate((xq_ref, xk_ref, xv_ref)):
        cs = slice(gi * GW, (gi + 1) * GW)
        acc = x_ref[...].astype(F32) * w_ref[GDN_CONV - 1:GDN_CONV, cs]
        for i in range(GDN_CONV - 1):
            acc = acc + cs_ref[:, i * CH + gi * GW:i * CH + (gi + 1) * GW] * w_ref[i:i + 1, cs]
        cu_ref[:, cs] = _silu(acc)
    beta = _sigmoid(b_ref[...])
    eg = jnp.exp(-jnp.exp(al_ref[...]) * _softplus(a_ref[...] + dt_ref[...]))
    nw = nw_ref[...]
    qo, ko, vo = 0, H * dk, 2 * H * dk
    ks, qs = [], []
    for h in range(H):
        q = _l2norm(cu_ref[:, qo + h * dk:qo + (h + 1) * dk]) * (dk ** -0.5)
        k = _l2norm(cu_ref[:, ko + h * dk:ko + (h + 1) * dk])
        ks.append(k)
        qs.append(q)
        for b in range(bb):
            kq = jnp.concatenate([k[b:b + 1], q[b:b + 1], jnp.zeros((6, dk), F32)], axis=0)
            r = _dot(kq.astype(BF16), s_ref[b, h].astype(BF16))
            ks_ref[h, b:b + 1, :] = r[0:1]
            qs_ref[h, b:b + 1, :] = r[1:2]
    vns = []
    for h in range(H):
        q, k = qs[h], ks[h]
        v = cu_ref[:, vo + h * dv:vo + (h + 1) * dv]
        e = eg[:, h:h + 1]
        v_new = beta[:, h:h + 1] * (v - e * ks_ref[h])
        o = e * qs_ref[h] + jnp.sum(q * k, axis=-1, keepdims=True) * v_new
        o = o * lax.rsqrt(jnp.mean(o * o, axis=-1, keepdims=True) + EPS) * nw
        gate = _silu(z_ref[:, h * dv:(h + 1) * dv].astype(F32))
        o_ref[:, h * dv:(h + 1) * dv] = (o * gate).astype(o_ref.dtype)
        vns.append(v_new)
    for h in range(H):
        kT = ks[h].T
        for b in range(bb):
            kc = jnp.broadcast_to(kT[:, b:b + 1], (dk, dv))
            so_ref[b, h] = s_ref[b, h] * eg[b:b + 1, h:h + 1] + kc * vns[h][b:b + 1]


def gdn_dec(z, zb, za, conv_state, w_gconv, a_log, dt_bias, norm_w, state, acc, l, dk, dv, col0, bb=8):
    B = z.shape[0]
    H = GDN_HEADS
    bb = _blk(B, bb)
    CH = 3 * H * dk
    GW = H * dk
    cb0 = col0 // GW
    body = functools.partial(_gdn_dec_body, bb=bb, dk=dk, dv=dv, CH=CH)
    st = pl.BlockSpec((None, bb, H, dk, dv), lambda i: (l, i, 0, 0, 0))
    hrow = pl.BlockSpec((bb, H), lambda i: (i, 0))
    hpar = pl.BlockSpec((None, 1, H), lambda i: (l, 0, 0))
    zspec = lambda off: pl.BlockSpec((bb, GW), lambda i: (i, cb0 + off))
    return pl.pallas_call(
        body, grid=(B // bb,),
        in_specs=[zspec(0), zspec(1), zspec(2), zspec(3),
                  pl.BlockSpec((None, bb, (GDN_CONV - 1) * CH), lambda i: (l, i, 0)),
                  pl.BlockSpec((None, GDN_CONV, CH), lambda i: (l, 0, 0)),
                  hrow, hrow, hpar, hpar,
                  pl.BlockSpec((None, 1, dv), lambda i: (l, 0, 0)), st, pl.BlockSpec(memory_space=pl.ANY)],
        out_specs=[pl.BlockSpec((bb, H * dv), lambda i: (i, 0)), st],
        out_shape=[jax.ShapeDtypeStruct((B, H * dv), BF16), jax.ShapeDtypeStruct(state.shape, F32)],
        input_output_aliases={12: 1},
        scratch_shapes=[pltpu.VMEM((bb, CH), F32), pltpu.VMEM((H, bb, dv), F32), pltpu.VMEM((H, bb, dv), F32)],
        compiler_params=_cparams("parallel"), name="gdn_dec",
    )(z, z, z, z, conv_state.reshape(-1, B, (GDN_CONV - 1) * CH), w_gconv, zb, za,
      a_log.reshape(-1, 1, H), dt_bias.reshape(-1, 1, H), norm_w.reshape(-1, 1, dv), state, acc)


def _softmax_rows(s):
    m = jnp.max(s, axis=-1, keepdims=True)
    e = jnp.exp(s - m)
    return e / jnp.sum(e, axis=-1, keepdims=True)


def _xattn_seq_body(q_ref, k_ref, v_ref, o_ref, *, hd):
    for h in range(X_HEADS):
        sl = slice(h * hd, (h + 1) * hd)
        s = _dot_nt(q_ref[:, sl], k_ref[0, :, sl].astype(BF16)) * (hd ** -0.5)
        a = _softmax_rows(s)
        o_ref[:, sl] = _dot(a.astype(BF16), v_ref[0, :, sl].astype(BF16)).astype(o_ref.dtype)


def xattn_seq(q, mk, mv, B, T, bq=512):
    W = q.shape[1]
    hd = W // X_HEADS
    bq = _blk(T, bq)
    nq = T // bq
    mem = pl.BlockSpec((1,) + mk.shape[1:], lambda b, j: (b, 0, 0))
    return pl.pallas_call(
        functools.partial(_xattn_seq_body, hd=hd), grid=(B, nq),
        in_specs=[pl.BlockSpec((bq, W), lambda b, j: (b * nq + j, 0)), mem, mem],
        out_specs=pl.BlockSpec((bq, W), lambda b, j: (b * nq + j, 0)),
        out_shape=jax.ShapeDtypeStruct((B * T, W), BF16),
        compiler_params=_cparams("parallel", "parallel"), name="xattn_seq",
    )(q, mk, mv)


def _xattn_dec_body(q_ref, k_hbm, v_hbm, o_ref, kbuf, vbuf, sem, *, l, bb, hd, nsteps):
    i = pl.program_id(0)
    W = X_HEADS * hd

    def copies(step, slot):
        out = []
        for t, (src, dst) in enumerate(((k_hbm, kbuf), (v_hbm, vbuf))):
            for h in range(X_HEADS):
                out.append(pltpu.make_async_copy(src.at[l, pl.ds(step * bb, bb), :, h, :],
                                                 dst.at[slot, :, :, pl.ds(h * hd, hd)], sem.at[t, slot, h]))
        return out

    slot = i % 2

    @pl.when(i == 0)
    def _():
        for cp in copies(0, 0):
            cp.start()

    @pl.when(i + 1 < nsteps)
    def _():
        for cp in copies(i + 1, 1 - slot):
            cp.start()

    for cp in copies(i, slot):
        cp.wait()

    q = q_ref[...]
    row = lax.broadcasted_iota(jnp.int32, (8, W), 0)
    head = lax.broadcasted_iota(jnp.int32, (8, W), 1) // hd
    for b in range(bb):
        qd = jnp.where(row == head, jnp.broadcast_to(q[b:b + 1].astype(F32), (8, W)), 0.0).astype(BF16)
        s = _dot_nt(qd, kbuf[slot, b].astype(BF16)) * (hd ** -0.5)
        a = _softmax_rows(s)
        o = _dot(a.astype(BF16), vbuf[slot, b].astype(BF16))
        o_ref[b:b + 1, :] = jnp.sum(jnp.where(row == head, o, 0.0), axis=0, keepdims=True).astype(o_ref.dtype)


def xattn_dec(q, mk, mv, l, bb=8):
    B, W = q.shape
    hd = W // X_HEADS
    n_mem = mk.shape[2]
    bb = _blk(B, bb)
    nsteps = B // bb
    hbm = pl.BlockSpec(memory_space=pl.ANY)
    return pl.pallas_call(
        functools.partial(_xattn_dec_body, l=l, bb=bb, hd=hd, nsteps=nsteps), grid=(nsteps,),
        in_specs=[pl.BlockSpec((bb, W), lambda i: (i, 0)), hbm, hbm],
        out_specs=pl.BlockSpec((bb, W), lambda i: (i, 0)),
        out_shape=jax.ShapeDtypeStruct((B, W), BF16),
        scratch_shapes=[pltpu.VMEM((2, bb, n_mem, W), F32), pltpu.VMEM((2, bb, n_mem, W), F32),
                        pltpu.SemaphoreType.DMA((2, 2, X_HEADS))],
        compiler_params=_cparams("arbitrary"), name="xattn_dec",
    )(q, mk, mv)


def _run_trunk(x, nseq, T, states, mem_k, mem_v, W, depth):
    D = x.shape[1]
    dk_r, dv_r = D // 32, D // 16
    dk_g = dv_g = D // 32
    n_main = W["w_in"].shape[2] - 2 * GDN_HEADS
    pool_w = W["w_pool"].shape[-1] * len(POOL_WINDOWS)
    off_pu = 2 * RET_HEADS * dk_r + 2 * RET_HEADS * dv_r
    off_c = off_pu + pool_w
    ch = 3 * GDN_HEADS * dk_g
    gdn_par = (W["w_gconv"], W["gdn_a_log"], W["gdn_dt_bias"], W["gdn_norm"])
    ffn_par = (W["w_gate"], W["w_up"], W["w_down"], W["w_fconv"], W["b_fconv"])
    outs = [[] for _ in range(5)]
    if states is not None:
        acc_ret, acc_gdn = lax.empty(states[0].shape, F32), lax.empty(states[2].shape, F32)
    hn = rmsnorm(x, W["norm_mix"], 0)
    for l in range(depth):
        z = matmul(hn, W["w_in"], l, out_dtype=BF16, n_out=n_main)
        zba = matmul(hn, W["w_ba"], l, out_dtype=F32)
        if states is None:
            ro, s_ret = retention_seq(z, nseq, T, dk_r, dv_r)
            po = pool_seq(z, W["w_pool"], W["ls_pool"], l, nseq, T, off_pu // pool_w)
            co, s_gdn = gdn_seq(z, zba, *gdn_par, l, nseq, T, dk_g, dv_g, off_c)
            z3 = z.reshape(nseq, T, n_main)
            s_pool = z3[:, T - POOL_BUF:, off_pu:off_pu + pool_w].astype(F32)
            s_gconv = z3[:, T - (GDN_CONV - 1):, off_c:off_c + ch].astype(F32)
        else:
            st_ret, st_pool, st_gdn, st_gconv, st_fconv = states
            zg = zba.reshape(zba.shape[0], GDN_HEADS // GDN_HB, LANES)
            zb = zg[:, :, :GDN_HB].reshape(-1, GDN_HEADS)
            za = zg[:, :, GDN_HB:2 * GDN_HB].reshape(-1, GDN_HEADS)
            ro, acc_ret = retention_dec(z, st_ret, acc_ret, l, dk_r, dv_r)
            po = pool_dec(z, st_pool, W["w_pool"], W["ls_pool"], l, off_pu // pool_w)
            co, acc_gdn = gdn_dec(z, zb, za, st_gconv, *gdn_par, st_gdn, acc_gdn, l, dk_g, dv_g, off_c)
            s_ret = s_gdn = None
            s_pool = jnp.concatenate([st_pool[l, :, 1:], z[:, None, off_pu:off_pu + pool_w].astype(F32)], axis=1)
            s_gconv = jnp.concatenate([st_gconv[l, :, 1:], z[:, None, off_c:off_c + ch].astype(F32)], axis=1)
        x = matmul([ro, po, co], W["w_out"], l, res=x)
        hx = rmsnorm(x, W["norm_x"], l)
        q = matmul(hx, W["w_xq"], l, out_dtype=BF16)
        ao = xattn_seq(q, mem_k[l], mem_v[l], nseq, T) if states is None else xattn_dec(q, mem_k, mem_v, l)
        x = matmul(ao, W["w_xo"], l, res=x)
        hf = rmsnorm(x, W["norm_ffn"], l)
        if states is None:
            y, s_fconv = ffn_seq(hf, *ffn_par, l, nseq)
        else:
            y, g_new = ffn_dec(hf, *ffn_par, st_fconv, l)
            s_fconv = jnp.concatenate([st_fconv[l, :, 1:], g_new[:, None, :]], axis=1)
        if l + 1 < depth:
            x, hn = add_rmsnorm(x, y, W["norm_mix"], l + 1)
        else:
            x, hn = add_rmsnorm(x, y, W["norm_f"], 0, out_dtype=F32)
        for lst, s in zip(outs, (s_ret, s_pool, s_gdn, s_gconv, s_fconv)):
            lst.append(s)
    if states is None:
        return hn, [jnp.stack(s) for s in outs]
    return hn, [acc_ret, jnp.stack(outs[1]), acc_gdn, jnp.stack(outs[3]), jnp.stack(outs[4])]


def kernel(x_prompt, x_sample, state_ret, state_pool, state_gdn, state_gdn_conv, state_ffn_conv, cache_mem_k, cache_mem_v, mem_prompt, norm_mix, w_in, w_pool, ls_pool, w_gconv, gdn_a_log, gdn_dt_bias, gdn_norm, w_out, norm_x, norm_mem, w_xq, w_xk, w_xv, w_xo, norm_ffn, w_gate, w_up, w_fconv, b_fconv, w_down, norm_f):
    depth = w_in.shape[0]
    Bp, T, D = x_prompt.shape
    Bs = x_sample.shape[0]
    assert x_sample.shape[1] == 1 and D % (32 * LANES) == 0
    n_in = w_in.shape[2]
    n_main = n_in - 2 * GDN_HEADS
    hg = GDN_HEADS // GDN_HB
    w_b = w_in[:, :, n_main:n_main + GDN_HEADS].reshape(depth, D, hg, GDN_HB)
    w_a = w_in[:, :, n_main + GDN_HEADS:].reshape(depth, D, hg, GDN_HB)
    w_ba = jnp.pad(jnp.concatenate([w_b, w_a], axis=-1), ((0, 0), (0, 0), (0, 0), (0, LANES - 2 * GDN_HB)))
    w_ba = w_ba.reshape(depth, D, hg * LANES)
    W = dict(norm_mix=norm_mix, norm_x=norm_x, norm_ffn=norm_ffn, norm_f=norm_f.reshape(1, D),
             w_in=w_in.astype(BF16), w_ba=w_ba.astype(BF16),
             w_pool=w_pool.astype(BF16), ls_pool=ls_pool, w_gconv=w_gconv,
             gdn_a_log=gdn_a_log, gdn_dt_bias=gdn_dt_bias, gdn_norm=gdn_norm,
             w_out=w_out.astype(BF16), w_xq=w_xq.astype(BF16), w_xo=w_xo.astype(BF16),
             w_gate=w_gate.astype(BF16), w_up=w_up.astype(BF16), w_down=w_down.astype(BF16),
             w_fconv=w_fconv, b_fconv=b_fconv)
    n_mem = mem_prompt.shape[1]
    memf = mem_prompt.reshape(Bp * n_mem, D)
    w_xk16, w_xv16 = w_xk.astype(BF16), w_xv.astype(BF16)
    pk, pv = [], []
    for l in range(depth):
        mn = rmsnorm(memf, norm_mem, l)
        pk.append(matmul(mn, w_xk16, l).reshape(Bp, n_mem, -1))
        pv.append(matmul(mn, w_xv16, l).reshape(Bp, n_mem, -1))
    y_p, p_st = _run_trunk(x_prompt.reshape(Bp * T, D), Bp, T, None, pk, pv, W, depth)
    y_s, s_st = _run_trunk(x_sample.reshape(Bs, D), Bs, 1,
                           (state_ret, state_pool, state_gdn, state_gdn_conv, state_ffn_conv),
                           cache_mem_k, cache_mem_v, W, depth)
    p_mem_k = jnp.stack(pk).reshape(depth, Bp, n_mem, X_HEADS, -1)
    p_mem_v = jnp.stack(pv).reshape(depth, Bp, n_mem, X_HEADS, -1)
    return (y_p.reshape(Bp, T, D), y_s.reshape(Bs, 1, D), *p_st, p_mem_k, p_mem_v, *s_st)
```
